```python
import jax, jax.numpy as jnp
from jax import lax
import numpy as np

D_MODEL = 2048
BATCH = 2
SEQ = 4096
DEPTH = 2
DEC_BATCH = 128
DEC_SEQ = 8
PAST_LEN = 16384
PAGE_SIZE = 128

N_BRANCH = 4
BRANCH_W = D_MODEL // N_BRANCH
D_FF = 5632
ROPE_THETA = 10000.0
RMS_EPS = 1e-6
Q_BLOCK = 128
NEG = -1e30
POOL_WINDOWS = (2, 4, 8, 16)
POOL_GW = BRANCH_W // len(POOL_WINDOWS)
POOL_BUF = max(POOL_WINDOWS) - 1
MLA_HEADS = 4
MLA_NOPE = 128
MLA_ROPE = 64
MLA_V = BRANCH_W // MLA_HEADS
MLA_Q_LORA = D_MODEL // 4
MLA_KV_LORA = D_MODEL // 8
MLA_CACHE_W = MLA_KV_LORA + MLA_ROPE
MLA_SCALE = (MLA_NOPE + MLA_ROPE) ** -0.5
NSA_HEADS = 8
NSA_HD = BRANCH_W // NSA_HEADS
NSA_BLOCK = 64
NSA_TOPN = 16
NSA_WINDOW = 512
NSA_SLOTS = 4
NSA_SCALE = NSA_HD ** -0.5
RWKV_HS = 64
RWKV_HEADS = BRANCH_W // RWKV_HS
RWKV_W_LORA = 64
RWKV_A_LORA = 64
RWKV_G_LORA = 128
RWKV_GN_EPS = 64e-5
RWKV_IN = 3 * BRANCH_W + RWKV_W_LORA + RWKV_A_LORA + RWKV_G_LORA
MLA_IN = MLA_Q_LORA + MLA_KV_LORA + MLA_ROPE
NSA_IN = BRANCH_W + 6 * NSA_HD + 3 * NSA_HEADS
OFF_MLA = BRANCH_W
OFF_NSA = OFF_MLA + MLA_IN
OFF_RWKV = OFF_NSA + NSA_IN
OFF_GATE = OFF_RWKV + RWKV_IN
N_IN = OFF_GATE + N_BRANCH * D_MODEL

kernel_name = 'hybrid_pool_mla_nsa_rwkv7_macaron_step'


def rmsnorm(x, g):
    xf = x.astype(jnp.float32)
    y = xf * lax.rsqrt(jnp.mean(xf * xf, -1, keepdims=True) + RMS_EPS)
    return (y * g.astype(jnp.float32)).astype(x.dtype)


def rope(x, pos):
    d = x.shape[-1]
    inv = ROPE_THETA ** (-jnp.arange(0, d, 2, dtype=jnp.float32) / d)
    ang = pos.astype(jnp.float32)[:, None] * inv[None, :]
    shape = (ang.shape[0],) + (1,) * (x.ndim - 3) + (d // 2,)
    cos = jnp.cos(ang).reshape(shape)
    sin = jnp.sin(ang).reshape(shape)
    xf = x.astype(jnp.float32)
    x1, x2 = xf[..., :d // 2], xf[..., d // 2:]
    return jnp.concatenate([x1 * cos - x2 * sin, x1 * sin + x2 * cos], -1).astype(x.dtype)


def masked_softmax(s, mask):
    s = jnp.where(mask, s.astype(jnp.float32), NEG)
    m = jnp.max(s, -1, keepdims=True)
    e = jnp.where(mask, jnp.exp(s - m), 0.0)
    den = jnp.sum(e, -1, keepdims=True)
    return e / jnp.where(den > 0, den, 1.0)


def swiglu(h, wg, wu, wd):
    return (jax.nn.silu(h @ wg) * (h @ wu)) @ wd


def _split_in(z):
    return (z[..., :OFF_MLA], z[..., OFF_MLA:OFF_NSA], z[..., OFF_NSA:OFF_RWKV],
            z[..., OFF_RWKV:OFF_GATE], z[..., OFF_GATE:])


def _merge(outs, z_gate, w_branch, w_out):
    b, t = z_gate.shape[:2]
    gates = jax.nn.sigmoid(z_gate.reshape(b, t, N_BRANCH, D_MODEL))
    proj = jnp.einsum('btnc,ncd->btnd', jnp.stack(outs, 2), w_branch)
    return jnp.sum(gates * proj, 2) @ w_out


def _pool_mix(u, prefix, pos, w_grp, scale):
    t = u.shape[1]
    ue = jnp.concatenate([prefix, u], 1).astype(jnp.float32)
    cs = jnp.concatenate([jnp.zeros_like(ue[:, :1]), jnp.cumsum(ue, 1)], 1)
    end = cs[:, POOL_BUF + 1:]
    outs = []
    for gi, w in enumerate(POOL_WINDOWS):
        sl = slice(gi * POOL_GW, (gi + 1) * POOL_GW)
        win = end[..., sl] - cs[:, POOL_BUF + 1 - w:POOL_BUF + 1 - w + t, sl]
        cnt = jnp.minimum(w, pos + 1).astype(jnp.float32)[None, :, None]
        diff = (win / cnt - ue[:, POOL_BUF:, sl]).astype(u.dtype)
        outs.append(jnp.einsum('btc,cd->btd', diff, w_grp[gi]))
    return jnp.concatenate(outs, -1) * scale


def _mla_project(zm, pos, lp):
    b, t = zm.shape[:2]
    c_q = rmsnorm(zm[..., :MLA_Q_LORA], lp['mla_q_norm'])
    c_kv = rmsnorm(zm[..., MLA_Q_LORA:MLA_Q_LORA + MLA_KV_LORA], lp['mla_kv_norm'])
    k_pe = rope(zm[..., MLA_Q_LORA + MLA_KV_LORA:], pos)
    q = (c_q @ lp['mla_w_uq']).reshape(b, t, MLA_HEADS, MLA_NOPE + MLA_ROPE)
    q_lat = jnp.einsum('bthd,chd->bthc', q[..., :MLA_NOPE], lp['mla_w_uk'])
    q_pe = rope(q[..., MLA_NOPE:], pos)
    return q_lat, q_pe, c_kv, k_pe


def _mla_scores(q_lat, q_pe, c_kv, k_pe):
    return (jnp.einsum('bthc,bsc->bhts', q_lat, c_kv)
            + jnp.einsum('bthr,bsr->bhts', q_pe, k_pe)) * MLA_SCALE


def _mla_up(o_lat, w_uv):
    b, t = o_lat.shape[:2]
    return jnp.einsum('bthc,chd->bthd', o_lat, w_uv).reshape(b, t, MLA_HEADS * MLA_V)


def _mla_prompt(q_lat, q_pe, c_kv, k_pe, w_uv):
    b, s = q_lat.shape[:2]
    nq = s // Q_BLOCK
    kpos = jnp.arange(s)

    def block(args):
        ql, qp, qpos = args
        p = masked_softmax(_mla_scores(ql, qp, c_kv, k_pe), (kpos[None, :] <= qpos[:, None])[None, None])
        return jnp.einsum('bhts,bsc->bthc', p.astype(c_kv.dtype), c_kv)

    blk = lambda a: a.reshape((b, nq, Q_BLOCK) + a.shape[2:]).swapaxes(0, 1)
    o_lat = lax.map(block, (blk(q_lat), blk(q_pe), kpos.reshape(nq, Q_BLOCK)))
    return _mla_up(o_lat.swapaxes(0, 1).reshape(b, s, MLA_HEADS, MLA_KV_LORA), w_uv)


def _mla_sample(q_lat, q_pe, c_kv, k_pe, c_past, pe_past, w_uv):
    t = q_lat.shape[1]
    n_past = c_past.shape[1]
    s_all = jnp.concatenate([_mla_scores(q_lat, q_pe, c_past, pe_past),
                             _mla_scores(q_lat, q_pe, c_kv, k_pe)], -1)
    causal = jnp.arange(t)[None, :] <= jnp.arange(t)[:, None]
    mask = jnp.concatenate([jnp.ones((t, n_past), bool), causal], 1)[None, None]
    p = masked_softmax(s_all, mask)
    o_lat = (jnp.einsum('bhts,bsc->bthc', p[..., :n_past].astype(c_past.dtype), c_past)
             + jnp.einsum('bhts,bsc->bthc', p[..., n_past:].astype(c_kv.dtype), c_kv))
    return _mla_up(o_lat, w_uv)


def _nsa_project(zn, pos):
    b, t = zn.shape[:2]
    q = rope(zn[..., :BRANCH_W].reshape(b, t, NSA_HEADS, NSA_HD), pos)
    kv = zn[..., BRANCH_W:BRANCH_W + 6 * NSA_HD].reshape(b, t, 6, NSA_HD)
    k_cmp, v_cmp, k_sel, v_sel, k_win, v_win = [kv[:, :, i] for i in range(6)]
    gates = jax.nn.sigmoid(zn[..., BRANCH_W + 6 * NSA_HD:].reshape(b, t, 3, NSA_HEADS))
    return q, k_cmp, v_cmp, rope(k_sel, pos), v_sel, rope(k_win, pos), v_win, gates


def _nsa_blocks(kc, vc, ks, vs, phi_k, phi_v):
    b, L, d = kc.shape
    nb = L // NSA_BLOCK

    def bmean(a):
        return jnp.mean(a[:, :nb * NSA_BLOCK].reshape(b, nb, NSA_BLOCK, d).astype(jnp.float32), 2).astype(a.dtype)

    bpos = jnp.arange(nb) * NSA_BLOCK + (NSA_BLOCK - 1)
    kb = rope(bmean(kc) @ phi_k, bpos)
    vb = bmean(vc) @ phi_v
    nsel = -(-L // NSA_BLOCK)
    padn = nsel * NSA_BLOCK - L
    sblocks = lambda a: jnp.pad(a, ((0, 0), (0, padn), (0, 0))).reshape(b, nsel, NSA_BLOCK, d)
    return kb, vb, bpos, sblocks(ks), sblocks(vs)


def _nsa_cmp_sel(q, qpos, kb, vb, bpos, ksb, vsb):
    b, t = q.shape[:2]
    nb = kb.shape[1]
    sc = jnp.einsum('bthd,bnd->bhtn', q, kb) * NSA_SCALE
    vis = (bpos[None, :] <= qpos[:, None])[None, None]
    p = masked_softmax(sc, vis)
    o_cmp = jnp.einsum('bhtn,bnd->bthd', p.astype(vb.dtype), vb)
    cur = qpos // NSA_BLOCK
    cand = (jnp.arange(nb)[None, :] < cur[:, None])[None]
    imp = jnp.where(cand, jnp.sum(p, 1), NEG)
    top_val, top_idx = lax.top_k(imp, min(NSA_TOPN - 1, nb))
    idx = jnp.concatenate([jnp.broadcast_to(cur[None, :, None], (b, t, 1)).astype(top_idx.dtype), top_idx], -1)
    valid = jnp.concatenate([jnp.ones((b, t, 1), bool), top_val > NEG / 2], -1)
    bi = jnp.arange(b)[:, None, None]
    kg = ksb[bi, idx]
    vg = vsb[bi, idx]
    n = idx.shape[-1]
    kpos = idx[..., None] * NSA_BLOCK + jnp.arange(NSA_BLOCK)
    m = valid[..., None] & (kpos <= qpos[None, :, None, None])
    ss = jnp.einsum('bthd,btnjd->bthnj', q, kg) * NSA_SCALE
    ps = masked_softmax(ss.reshape(b, t, NSA_HEADS, n * NSA_BLOCK), m.reshape(b, t, 1, n * NSA_BLOCK))
    o_sel = jnp.einsum('bthm,btmd->bthd', ps.astype(vg.dtype), vg.reshape(b, t, n * NSA_BLOCK, NSA_HD))
    return o_cmp, o_sel


def _nsa_window_prompt(q, kw, vw):
    b, s = q.shape[:2]
    nq = s // Q_BLOCK
    nband = NSA_WINDOW // Q_BLOCK + 1
    pad = jnp.zeros((b, NSA_WINDOW, NSA_HD), kw.dtype)
    kp = jnp.concatenate([pad, kw], 1).reshape(b, nband - 1 + nq, Q_BLOCK, NSA_HD)
    vp = jnp.concatenate([pad, vw], 1).reshape(b, nband - 1 + nq, Q_BLOCK, NSA_HD)
    kband = jnp.concatenate([kp[:, i:i + nq] for i in range(nband)], 2)
    vband = jnp.concatenate([vp[:, i:i + nq] for i in range(nband)], 2)
    qpos = jnp.arange(s).reshape(nq, Q_BLOCK)
    kpos = qpos[:, :1] - NSA_WINDOW + jnp.arange(nband * Q_BLOCK)[None]
    rel = qpos[:, :, None] - kpos[:, None, :]
    mask = (rel >= 0) & (rel < NSA_WINDOW) & (kpos[:, None, :] >= 0)
    qb = q.reshape(b, nq, Q_BLOCK, NSA_HEADS, NSA_HD)
    sc = jnp.einsum('bqthd,bqkd->bqhtk', qb, kband) * NSA_SCALE
    p = masked_softmax(sc, mask[None, :, None])
    o = jnp.einsum('bqhtk,bqkd->bqthd', p.astype(vband.dtype), vband)
    return o.reshape(b, s, NSA_HEADS, NSA_HD)


def _nsa_window_sample(q, qpos, kw, vw, buf):
    t = q.shape[1]
    lw = buf.shape[1]
    kall = jnp.concatenate([buf[:, :, 0], kw], 1)
    vall = jnp.concatenate([buf[:, :, 1], vw], 1)
    kpos = qpos[0] - lw + jnp.arange(lw + t)
    rel = qpos[:, None] - kpos[None, :]
    mask = ((rel >= 0) & (rel < NSA_WINDOW))[None, None]
    p = masked_softmax(jnp.einsum('bthd,bkd->bhtk', q, kall) * NSA_SCALE, mask)
    o = jnp.einsum('bhtk,bkd->bthd', p.astype(vall.dtype), vall)
    return o, jnp.stack([kall[:, -lw:], vall[:, -lw:]], 2)


def _nsa_gate(g, o_cmp, o_sel, o_win):
    b, t = g.shape[:2]
    o = g[:, :, 0, :, None] * o_cmp + g[:, :, 1, :, None] * o_sel + g[:, :, 2, :, None] * o_win
    return o.reshape(b, t, BRANCH_W)


def _rwkv_mix(zr, shift_prev, s0, lp):
    b, t, _ = zr.shape
    C, H, N = BRANCH_W, RWKV_HEADS, RWKV_HS
    zprev = jnp.concatenate([shift_prev[:, None], zr[:, :-1]], 1)
    zs = zr + (zprev - zr) * lp['rwkv_mu']
    r, k, v = zs[..., :C], zs[..., C:2 * C], zs[..., 2 * C:3 * C]
    o = 3 * C
    wl = zs[..., o:o + RWKV_W_LORA]
    al = zs[..., o + RWKV_W_LORA:o + RWKV_W_LORA + RWKV_A_LORA]
    gl = zs[..., o + RWKV_W_LORA + RWKV_A_LORA:]
    w = -jax.nn.softplus(-(lp['rwkv_w0'] + jnp.tanh(wl) @ lp['rwkv_w2'])) - 0.5
    decay = jnp.exp(-jnp.exp(w.astype(jnp.float32)))
    a = jax.nn.sigmoid(lp['rwkv_a0'] + al @ lp['rwkv_a2'])
    g = jax.nn.sigmoid(gl) @ lp['rwkv_g2']
    hd = lambda u: u.reshape(b, t, H, N).astype(jnp.float32)
    kk = hd(k * lp['rwkv_k_k'])
    kk = kk / jnp.maximum(jnp.sqrt(jnp.sum(kk * kk, -1, keepdims=True)), 1e-12)
    k = k * (1 + (a - 1) * lp['rwkv_k_a'])
    rh, kh, vh, ah = hd(r), hd(k), hd(v), hd(a)
    dh = decay.reshape(b, t, H, N)

    def step(S, inp):
        r_t, k_t, v_t, d_t, kk_t, a_t = inp
        sa = jnp.einsum('bhvk,bhk->bhv', S, kk_t)
        S = (S * d_t[:, :, None, :] - sa[..., None] * (kk_t * a_t)[:, :, None, :]
             + v_t[..., None] * k_t[:, :, None, :])
        return S, jnp.einsum('bhvk,bhk->bhv', S, r_t)

    tm = lambda u: u.swapaxes(0, 1)
    s_fin, ys = lax.scan(step, s0.astype(jnp.float32), (tm(rh), tm(kh), tm(vh), tm(dh), tm(kk), tm(ah)))
    y = ys.swapaxes(0, 1)
    mu = jnp.mean(y, -1, keepdims=True)
    var = jnp.mean(jnp.square(y - mu), -1, keepdims=True)
    y = ((y - mu) * lax.rsqrt(var + RWKV_GN_EPS)).reshape(b, t, C) * lp['rwkv_ln_w'] + lp['rwkv_ln_b']
    bonus = jnp.sum(rh * kh * lp['rwkv_r_k'], -1, keepdims=True) * vh
    y = (y + bonus.reshape(b, t, C)) * g
    return y.astype(zr.dtype), s_fin


def _mixer_prompt(h, lp, win_len):
    b, s = h.shape[:2]
    pos = jnp.arange(s)
    z = h @ lp['w_in']
    zp, zm, zn, zr, zg = _split_in(z)
    o_pool = _pool_mix(zp, jnp.zeros((b, POOL_BUF, BRANCH_W), z.dtype), pos, lp['pool_w'], lp['pool_scale'])
    q_lat, q_pe, c_kv, k_pe = _mla_project(zm, pos, lp)
    o_mla = _mla_prompt(q_lat, q_pe, c_kv, k_pe, lp['mla_w_uv'])
    q, kc, vc, ks, vs, kw, vw, g = _nsa_project(zn, pos)
    kb, vb, bpos, ksb, vsb = _nsa_blocks(kc, vc, ks, vs, lp['nsa_phi_k'], lp['nsa_phi_v'])
    nq = s // Q_BLOCK
    qb = q.reshape(b, nq, Q_BLOCK, NSA_HEADS, NSA_HD).swapaxes(0, 1)
    o_cmp, o_sel = lax.map(lambda a: _nsa_cmp_sel(a[0], a[1], kb, vb, bpos, ksb, vsb),
                           (qb, pos.reshape(nq, Q_BLOCK)))
    unblk = lambda a: a.swapaxes(0, 1).reshape(b, s, NSA_HEADS, NSA_HD)
    o_nsa = _nsa_gate(g, unblk(o_cmp), unblk(o_sel), _nsa_window_prompt(q, kw, vw))
    o_rwkv, s_fin = _rwkv_mix(zr, jnp.zeros((b, RWKV_IN), z.dtype),
                              jnp.zeros((b, RWKV_HEADS, RWKV_HS, RWKV_HS), jnp.float32), lp)
    out = _merge((o_pool, o_mla, o_nsa, o_rwkv), zg, lp['w_branch'], lp['w_out'])
    kv_win = jnp.stack([kw, vw], 2)
    win_state = jnp.concatenate([jnp.zeros((b, win_len, 2, NSA_HD), z.dtype), kv_win], 1)[:, -win_len:]
    new = (jnp.concatenate([c_kv, k_pe], -1), jnp.stack([kc, vc, ks, vs], 2), win_state,
           zp[:, -POOL_BUF:], s_fin.astype(z.dtype), zr[:, -1])
    return out, new


def _mixer_sample(h, lp, layer, cache_mla, cache_nsa, page_table, win_buf, pool_buf, rwkv_s, rwkv_shift):
    db, t = h.shape[:2]
    past = page_table.shape[1] * cache_mla.shape[2]
    pos = past + jnp.arange(t)
    z = h @ lp['w_in']
    zp, zm, zn, zr, zg = _split_in(z)
    o_pool = _pool_mix(zp, pool_buf, pos, lp['pool_w'], lp['pool_scale'])
    new_pool = jnp.concatenate([pool_buf, zp], 1)[:, -POOL_BUF:]
    q_lat, q_pe, c_kv, k_pe = _mla_project(zm, pos, lp)
    c_past = cache_mla[layer, page_table, :, :MLA_KV_LORA].reshape(db, past, MLA_KV_LORA)
    pe_past = cache_mla[layer, page_table, :, MLA_KV_LORA:].reshape(db, past, MLA_ROPE)
    o_mla = _mla_sample(q_lat, q_pe, c_kv, k_pe, c_past, pe_past, lp['mla_w_uv'])
    q, kc, vc, ks, vs, kw, vw, g = _nsa_project(zn, pos)
    kc_f, vc_f, ks_f, vs_f = [jnp.concatenate([cache_nsa[layer, page_table, :, i].reshape(db, past, NSA_HD), nw], 1)
                              for i, nw in enumerate((kc, vc, ks, vs))]
    kb, vb, bpos, ksb, vsb = _nsa_blocks(kc_f, vc_f, ks_f, vs_f, lp['nsa_phi_k'], lp['nsa_phi_v'])
    o_cmp, o_sel = _nsa_cmp_sel(q, pos, kb, vb, bpos, ksb, vsb)
    o_win, new_win = _nsa_window_sample(q, pos, kw, vw, win_buf)
    o_nsa = _nsa_gate(g, o_cmp, o_sel, o_win)
    o_rwkv, s_fin = _rwkv_mix(zr, rwkv_shift, rwkv_s, lp)
    out = _merge((o_pool, o_mla, o_nsa, o_rwkv), zg, lp['w_branch'], lp['w_out'])
    new = (jnp.concatenate([c_kv, k_pe], -1), jnp.stack([kc, vc, ks, vs], 2), new_win,
           new_pool, s_fin.astype(rwkv_s.dtype), zr[:, -1])
    return out, new


def setup_inputs(seed: int = 0) -> dict:
    key = jax.random.key(seed)
    keys = iter(jax.random.split(key, 64))
    f32 = jnp.float32
    nrm = lambda shape, scale: jax.random.normal(next(keys), shape, f32) * scale
    gain = lambda shape: 1.0 + 0.1 * jax.random.normal(next(keys), shape, f32)
    L = DEPTH
    n_pages = PAST_LEN // PAGE_SIZE
    n_used = DEC_BATCH * n_pages
    n_phys = n_used + (n_used + 3) // 4
    win_len = min(NSA_WINDOW, PAST_LEN)
    page_table = jax.random.permutation(next(keys), n_phys)[:n_used].reshape(DEC_BATCH, n_pages).astype(jnp.int32)
    return {
        'x_prompt': nrm((BATCH, SEQ, D_MODEL), 1.0),
        'x_sample': nrm((DEC_BATCH, DEC_SEQ, D_MODEL), 1.0),
        'cache_mla': jax.random.normal(next(keys), (L, n_phys, PAGE_SIZE, MLA_CACHE_W), f32),
        'cache_nsa': jax.random.normal(next(keys), (L, n_phys, PAGE_SIZE, NSA_SLOTS, NSA_HD), f32),
        'state_nsa_win': nrm((L, DEC_BATCH, win_len, 2, NSA_HD), 1.0),
        'state_pool': nrm((L, DEC_BATCH, POOL_BUF, BRANCH_W), 1.0),
        'state_rwkv': nrm((L, DEC_BATCH, RWKV_HEADS, RWKV_HS, RWKV_HS), 0.3),
        'state_rwkv_shift': nrm((L, DEC_BATCH, RWKV_IN), 1.0),
        'page_table': page_table,
        'norm_ffn1': gain((L, D_MODEL)),
        'ffn1_w_gate': nrm((L, D_MODEL, D_FF), D_MODEL ** -0.5),
        'ffn1_w_up': nrm((L, D_MODEL, D_FF), D_MODEL ** -0.5),
        'ffn1_w_down': nrm((L, D_FF, D_MODEL), D_FF ** -0.5),
        'norm_mix': gain((L, D_MODEL)),
        'w_in': nrm((L, D_MODEL, N_IN), D_MODEL ** -0.5),
        'w_branch': nrm((L, N_BRANCH, BRANCH_W, D_MODEL), BRANCH_W ** -0.5),
        'w_out': nrm((L, D_MODEL, D_MODEL), D_MODEL ** -0.5),
        'pool_w': nrm((L, len(POOL_WINDOWS), POOL_GW, POOL_GW), POOL_GW ** -0.5),
        'pool_scale': gain((L, BRANCH_W)),
        'mla_q_norm': gain((L, MLA_Q_LORA)),
        'mla_w_uq': nrm((L, MLA_Q_LORA, MLA_HEADS * (MLA_NOPE + MLA_ROPE)), MLA_Q_LORA ** -0.5),
        'mla_kv_norm': gain((L, MLA_KV_LORA)),
        'mla_w_uk': nrm((L, MLA_KV_LORA, MLA_HEADS, MLA_NOPE), MLA_KV_LORA ** -0.5),
        'mla_w_uv': nrm((L, MLA_KV_LORA, MLA_HEADS, MLA_V), MLA_KV_LORA ** -0.5),
        'nsa_phi_k': nrm((L, NSA_HD, NSA_HD), NSA_HD ** -0.5),
        'nsa_phi_v': nrm((L, NSA_HD, NSA_HD), NSA_HD ** -0.5),
        'rwkv_mu': jax.random.uniform(next(keys), (L, RWKV_IN), f32),
        'rwkv_w0': -1.0 + nrm((L, BRANCH_W), 0.5),
        'rwkv_w2': nrm((L, RWKV_W_LORA, BRANCH_W), 0.1),
        'rwkv_a0': nrm((L, BRANCH_W), 0.1),
        'rwkv_a2': nrm((L, RWKV_A_LORA, BRANCH_W), RWKV_A_LORA ** -0.5),
        'rwkv_g2': nrm((L, RWKV_G_LORA, BRANCH_W), RWKV_G_LORA ** -0.5),
        'rwkv_k_k': 0.85 + nrm((L, BRANCH_W), 0.05),
        'rwkv_k_a': 1.0 + nrm((L, BRANCH_W), 0.05),
        'rwkv_r_k': nrm((L, RWKV_HEADS, RWKV_HS), 0.1),
        'rwkv_ln_w': gain((L, BRANCH_W)),
        'rwkv_ln_b': nrm((L, BRANCH_W), 0.02),
        'norm_ffn2': gain((L, D_MODEL)),
        'ffn2_w_gate': nrm((L, D_MODEL, D_FF), D_MODEL ** -0.5),
        'ffn2_w_up': nrm((L, D_MODEL, D_FF), D_MODEL ** -0.5),
        'ffn2_w_down': nrm((L, D_FF, D_MODEL), D_FF ** -0.5),
        'norm_final': gain((D_MODEL,)),
    }


def reference(x_prompt, x_sample, cache_mla, cache_nsa, state_nsa_win, state_pool, state_rwkv,
              state_rwkv_shift, page_table, norm_ffn1, ffn1_w_gate, ffn1_w_up, ffn1_w_down,
              norm_mix, w_in, w_branch, w_out, pool_w, pool_scale, mla_q_norm, mla_w_uq,
              mla_kv_norm, mla_w_uk, mla_w_uv, nsa_phi_k, nsa_phi_v, rwkv_mu, rwkv_w0, rwkv_w2,
              rwkv_a0, rwkv_a2, rwkv_g2, rwkv_k_k, rwkv_k_a, rwkv_r_k, rwkv_ln_w, rwkv_ln_b,
              norm_ffn2, ffn2_w_gate, ffn2_w_up, ffn2_w_down, norm_final):
    win_len = state_nsa_win.shape[2]
    xp, xs = x_prompt, x_sample
    new_p = [[] for _ in range(6)]
    new_s = [[] for _ in range(6)]
    for l in range(DEPTH):
        lp = {'w_in': w_in[l], 'w_branch': w_branch[l], 'w_out': w_out[l],
              'pool_w': pool_w[l], 'pool_scale': pool_scale[l],
              'mla_q_norm': mla_q_norm[l], 'mla_w_uq': mla_w_uq[l], 'mla_kv_norm': mla_kv_norm[l],
              'mla_w_uk': mla_w_uk[l], 'mla_w_uv': mla_w_uv[l],
              'nsa_phi_k': nsa_phi_k[l], 'nsa_phi_v': nsa_phi_v[l],
              'rwkv_mu': rwkv_mu[l], 'rwkv_w0': rwkv_w0[l], 'rwkv_w2': rwkv_w2[l],
              'rwkv_a0': rwkv_a0[l], 'rwkv_a2': rwkv_a2[l], 'rwkv_g2': rwkv_g2[l],
              'rwkv_k_k': rwkv_k_k[l], 'rwkv_k_a': rwkv_k_a[l], 'rwkv_r_k': rwkv_r_k[l],
              'rwkv_ln_w': rwkv_ln_w[l], 'rwkv_ln_b': rwkv_ln_b[l]}
        f1 = (ffn1_w_gate[l], ffn1_w_up[l], ffn1_w_down[l])
        f2 = (ffn2_w_gate[l], ffn2_w_up[l], ffn2_w_down[l])
        xp = xp + 0.5 * swiglu(rmsnorm(xp, norm_ffn1[l]), *f1)
        xs = xs + 0.5 * swiglu(rmsnorm(xs, norm_ffn1[l]), *f1)
        dp, st_p = _mixer_prompt(rmsnorm(xp, norm_mix[l]), lp, win_len)
        ds, st_s = _mixer_sample(rmsnorm(xs, norm_mix[l]), lp, l, cache_mla, cache_nsa, page_table,
                                 state_nsa_win[l], state_pool[l], state_rwkv[l], state_rwkv_shift[l])
        xp = xp + dp
        xs = xs + ds
        xp = xp + 0.5 * swiglu(rmsnorm(xp, norm_ffn2[l]), *f2)
        xs = xs + 0.5 * swiglu(rmsnorm(xs, norm_ffn2[l]), *f2)
        for i in range(6):
            new_p[i].append(st_p[i])
            new_s[i].append(st_s[i])
    y_prompt = rmsnorm(xp, norm_final)
    y_sample = rmsnorm(xs, norm_final)
    mla_p, nsa_p, win_p, pool_p, rwkv_p, shift_p = [jnp.stack(a, 0) for a in new_p]
    mla_s, nsa_s, win_s, pool_s, rwkv_s, shift_s = [jnp.stack(a, 0) for a in new_s]
    return (y_prompt, y_sample, mla_p, mla_s, nsa_p, nsa_s, win_p, win_s,
            pool_p, pool_s, rwkv_p, rwkv_s, shift_p, shift_s)
```

```python
import functools

import jax
import jax.numpy as jnp
from jax import lax
from jax.experimental import pallas as pl
from jax.experimental.pallas import tpu as pltpu

F32 = jnp.float32
BF16 = jnp.bfloat16

D_MODEL = 2048
N_BRANCH = 4
BRANCH_W = D_MODEL // N_BRANCH
ROPE_THETA = 10000.0
RMS_EPS = 1e-6
NEG = -1e30
POOL_WINDOWS = (2, 4, 8, 16)
POOL_GW = BRANCH_W // len(POOL_WINDOWS)
POOL_BUF = max(POOL_WINDOWS) - 1
POOL_HALO = 16
MLA_HEADS = 4
MLA_NOPE = 128
MLA_ROPE = 64
MLA_V = BRANCH_W // MLA_HEADS
MLA_Q_LORA = D_MODEL // 4
MLA_KV_LORA = D_MODEL // 8
MLA_CACHE_W = MLA_KV_LORA + MLA_ROPE
MLA_ROW_W = 384
MLA_SCALE = (MLA_NOPE + MLA_ROPE) ** -0.5
NSA_HEADS = 8
NSA_HD = BRANCH_W // NSA_HEADS
NSA_BLOCK = 64
NSA_TOPN = 16
NSA_WINDOW = 512
NSA_SCALE = NSA_HD ** -0.5
RWKV_HS = 64
RWKV_HEADS = BRANCH_W // RWKV_HS
RWKV_W_LORA = 64
RWKV_A_LORA = 64
RWKV_G_LORA = 128
RWKV_GN_EPS = 64e-5
RWKV_IN = 3 * BRANCH_W + RWKV_W_LORA + RWKV_A_LORA + RWKV_G_LORA
SCAN_ROWS = 256
Q_TILE = 128
LANE = 128

Z_GATE = 0
Z_POOL = N_BRANCH * D_MODEL
Z_MQ = Z_POOL + BRANCH_W
Z_MKV = Z_MQ + MLA_Q_LORA
Z_NQ = Z_MKV + MLA_ROW_W
Z_NKV = Z_NQ + BRANCH_W
Z_NG = Z_NKV + 6 * NSA_HD
Z_RW = Z_NG + LANE
Z_END = Z_RW + RWKV_IN
Z_TILE = 512
Z_W = -(-Z_END // Z_TILE) * Z_TILE


def _cparams(sem, vmem_mb=48):
    return pltpu.CompilerParams(dimension_semantics=sem, vmem_limit_bytes=vmem_mb * 1024 * 1024)


def _pick_tile(n, cands):
    for c in cands:
        if n % c == 0:
            return c
    raise ValueError(f"no tile in {cands} divides {n}")


def _vec3(a):
    return a.reshape(a.shape[0], 1, a.shape[1])


def _vec_spec(w, l, ngrid):
    if ngrid == 1:
        return pl.BlockSpec((None, 1, w), lambda i: (l, 0, 0))
    return pl.BlockSpec((None, 1, w), lambda i, j: (l, 0, 0))


def _dot(a, b):
    return jnp.dot(a.astype(BF16), b.astype(BF16), preferred_element_type=F32)


def _dot_nt(a, b):
    return lax.dot_general(a.astype(BF16), b.astype(BF16), (((1,), (1,)), ((), ())),
                           preferred_element_type=F32)


def _dot_split(a, b01, terms=2):
    acc = None
    rem = a
    for _ in range(terms):
        piece = rem.astype(BF16)
        d = jnp.dot(piece, b01, preferred_element_type=F32)
        acc = d if acc is None else acc + d
        rem = rem - piece.astype(F32)
    return acc


def _rms(x, g):
    return x * lax.rsqrt(jnp.mean(x * x, -1, keepdims=True) + RMS_EPS) * g


def _rope_apply(x, cos, sin):
    w = x.shape[-1]
    lane = lax.broadcasted_iota(jnp.int32, x.shape, x.ndim - 1)
    fwd = pltpu.roll(x, w - 32, x.ndim - 1)
    bwd = pltpu.roll(x, 32, x.ndim - 1)
    return x * cos + jnp.where((lane & 63) < 32, fwd, bwd) * sin


def _ffn_body(x_ref, g_ref, wg_ref, wu_ref, wd_ref, gf_ref, o_ref, h_ref, acc_ref, *, nj, final_norm):
    j = pl.program_id(1)

    @pl.when(j == 0)
    def _():
        h_ref[...] = _rms(x_ref[...], g_ref[...]).astype(BF16)

    h = h_ref[...]
    gt = jnp.dot(h, wg_ref[...], preferred_element_type=F32)
    up = jnp.dot(h, wu_ref[...], preferred_element_type=F32)
    d = _dot(gt * jax.nn.sigmoid(gt) * up, wd_ref[...])

    @pl.when(j == 0)
    def _():
        acc_ref[...] = d

    @pl.when(j > 0)
    def _():
        acc_ref[...] += d

    @pl.when(j == nj - 1)
    def _():
        y = x_ref[...] + 0.5 * acc_ref[...]
        if final_norm:
            y = _rms(y, gf_ref[...])
        o_ref[...] = y


def _ffn(x, g, wg, wu, wd, gf, l, final_norm):
    n, d = x.shape
    f = wg.shape[2]
    tm = _pick_tile(n, (512, 256, 128))
    tf = _pick_tile(f, (512, 256, 128))
    nj = f // tf
    return pl.pallas_call(
        functools.partial(_ffn_body, nj=nj, final_norm=final_norm),
        grid=(n // tm, nj),
        in_specs=[
            pl.BlockSpec((tm, d), lambda i, j: (i, 0)),
            pl.BlockSpec((1, d), lambda i, j: (0, 0)),
            pl.BlockSpec((None, d, tf), lambda i, j: (l, 0, j)),
            pl.BlockSpec((None, d, tf), lambda i, j: (l, 0, j)),
            pl.BlockSpec((None, tf, d), lambda i, j: (l, j, 0)),
            pl.BlockSpec((1, d), lambda i, j: (0, 0)),
        ],
        out_specs=pl.BlockSpec((tm, d), lambda i, j: (i, 0)),
        out_shape=jax.ShapeDtypeStruct((n, d), F32),
        scratch_shapes=[pltpu.VMEM((tm, d), BF16), pltpu.VMEM((tm, d), F32)],
        compiler_params=_cparams(("parallel", "arbitrary")),
        name="ffn",
    )(x, g.reshape(1, d), wg, wu, wd, gf.reshape(1, d))


def _inproj_body(x_ref, g_ref, w_ref, o_ref, h_ref):
    @pl.when(pl.program_id(1) == 0)
    def _():
        h_ref[...] = _rms(x_ref[...], g_ref[...]).astype(BF16)

    o_ref[...] = jnp.dot(h_ref[...], w_ref[...], preferred_element_type=F32)


def _inproj(x, g, w, l):
    n, d = x.shape
    zw = w.shape[2]
    tm = _pick_tile(n, (512, 256, 128))
    return pl.pallas_call(
        _inproj_body,
        grid=(n // tm, zw // Z_TILE),
        in_specs=[
            pl.BlockSpec((tm, d), lambda i, j: (i, 0)),
            pl.BlockSpec((1, d), lambda i, j: (0, 0)),
            pl.BlockSpec((None, d, Z_TILE), lambda i, j: (l, 0, j)),
        ],
        out_specs=pl.BlockSpec((tm, Z_TILE), lambda i, j: (i, j)),
        out_shape=jax.ShapeDtypeStruct((n, zw), F32),
        scratch_shapes=[pltpu.VMEM((tm, d), BF16)],
        compiler_params=_cparams(("parallel", "arbitrary")),
        name="inproj",
    )(x, g.reshape(1, d), w)


def _merge_body(x_ref, br_ref, zg_ref, wb_ref, wo_ref, o_ref, acc_ref):
    n = pl.program_id(1)
    t = jax.nn.sigmoid(zg_ref[...]) * _dot(br_ref[...], wb_ref[...])

    @pl.when(n == 0)
    def _():
        acc_ref[...] = t

    @pl.when(n > 0)
    def _():
        acc_ref[...] += t

    @pl.when(n == N_BRANCH - 1)
    def _():
        o_ref[...] = x_ref[...] + _dot(acc_ref[...], wo_ref[...])


def _merge(x, branches, z, wb, wo, l):
    n, d = x.shape
    tm = _pick_tile(n, (256, 128))
    return pl.pallas_call(
        _merge_body,
        grid=(n // tm, N_BRANCH),
        in_specs=[
            pl.BlockSpec((tm, d), lambda i, k: (i, 0)),
            pl.BlockSpec((tm, BRANCH_W), lambda i, k: (i, k)),
            pl.BlockSpec((tm, d), lambda i, k: (i, k)),
            pl.BlockSpec((None, None, BRANCH_W, d), lambda i, k: (l, k, 0, 0)),
            pl.BlockSpec((None, d, d), lambda i, k: (l, 0, 0)),
        ],
        out_specs=pl.BlockSpec((tm, d), lambda i, k: (i, 0)),
        out_shape=jax.ShapeDtypeStruct((n, d), F32),
        scratch_shapes=[pltpu.VMEM((tm, d), F32)],
        compiler_params=_cparams(("parallel", "arbitrary")),
        name="merge",
    )(x, branches, z, wb, wo)


def _pool_body(u_ref, w_ref, sc_ref, o_ref, *, t_len, t_chunk, pos0):
    g = pl.program_id(1)
    bt = u_ref.shape[0]
    for gi, win in enumerate(POOL_WINDOWS):
        @pl.when(g == gi)
        def _(win=win):
            for c0 in range(0, t_len, t_chunk):
                ue = u_ref[:, c0:c0 + t_chunk + POOL_HALO, :]
                acc = ue
                span = 1
                while span < win:
                    acc = acc[:, span:, :] + acc[:, :-span, :]
                    span *= 2
                wsum = acc[:, POOL_HALO + 1 - win:POOL_HALO + 1 - win + t_chunk, :]
                pos = pos0 + c0 + lax.broadcasted_iota(jnp.int32, (1, t_chunk, 1), 1)
                cnt = jnp.minimum(win, pos + 1).astype(F32)
                diff = wsum / cnt - ue[:, POOL_HALO:, :]
                y = _dot(diff.reshape(bt * t_chunk, LANE), w_ref[...]) * sc_ref[...]
                o_ref[:, c0:c0 + t_chunk, :] = y.reshape(bt, t_chunk, LANE)


def _pool(u_ext, w, scale, l, pos0, bt):
    b, te, c = u_ext.shape
    t_len = te - POOL_HALO
    t_chunk = _pick_tile(t_len, (512, 256, 128, 8))
    return pl.pallas_call(
        functools.partial(_pool_body, t_len=t_len, t_chunk=t_chunk, pos0=pos0),
        grid=(b // bt, len(POOL_WINDOWS)),
        in_specs=[
            pl.BlockSpec((bt, te, LANE), lambda i, g: (i, 0, g)),
            pl.BlockSpec((None, None, POOL_GW, POOL_GW), lambda i, g: (l, g, 0, 0)),
            pl.BlockSpec((None, 1, LANE), lambda i, g: (l, 0, g)),
        ],
        out_specs=pl.BlockSpec((bt, t_len, LANE), lambda i, g: (i, 0, g)),
        out_shape=jax.ShapeDtypeStruct((b, t_len, c), F32),
        compiler_params=_cparams(("parallel", "arbitrary")),
        name="pool",
    )(u_ext, w, _vec3(scale))


def _mla_proj_body(zq_ref, zkv_ref, qn_ref, kvn_ref, wuq_ref, wuk_ref, cq_ref, sq_ref, ckv_ref, skv_ref,
                   qcat_ref, rows_ref, rowsb_ref):
    tm = zq_ref.shape[0]
    zk = zkv_ref[...]
    ckv = _rms(zk[:, :MLA_KV_LORA], kvn_ref[...])
    roped = _rope_apply(zk, ckv_ref[...], skv_ref[...])
    rows = jnp.concatenate([ckv, roped[:, MLA_KV_LORA:]], -1)
    rows_ref[...] = rows
    rowsb_ref[...] = rows.astype(BF16)
    q = _dot(_rms(zq_ref[...], qn_ref[...]), wuq_ref[...])
    q = _rope_apply(q, cq_ref[...], sq_ref[...])
    pad = jnp.zeros((tm, MLA_ROW_W - MLA_CACHE_W), F32)
    for h in range(MLA_HEADS):
        q_lat = _dot(q[:, h * MLA_NOPE:(h + 1) * MLA_NOPE], wuk_ref[h])
        q_pe = q[:, MLA_HEADS * MLA_NOPE + h * MLA_ROPE:MLA_HEADS * MLA_NOPE + (h + 1) * MLA_ROPE]
        qcat_ref[0, h] = (jnp.concatenate([q_lat, q_pe, pad], -1) * MLA_SCALE).astype(BF16)


def _mla_proj(zq, zkv, qn, kvn, wuq, wuk, tabs, l):
    n = zq.shape[0]
    tm = Q_TILE
    qw = wuq.shape[2]
    row = lambda w: pl.BlockSpec((tm, w), lambda i: (i, 0))
    return pl.pallas_call(
        _mla_proj_body,
        grid=(n // tm,),
        in_specs=[
            row(MLA_Q_LORA), row(MLA_ROW_W),
            _vec_spec(MLA_Q_LORA, l, 1), _vec_spec(MLA_KV_LORA, l, 1),
            pl.BlockSpec((None, MLA_Q_LORA, qw), lambda i: (l, 0, 0)),
            pl.BlockSpec((None, MLA_HEADS, MLA_NOPE, MLA_KV_LORA), lambda i: (l, 0, 0, 0)),
            row(qw), row(qw), row(MLA_ROW_W), row(MLA_ROW_W),
        ],
        out_specs=[
            pl.BlockSpec((1, MLA_HEADS, tm, MLA_ROW_W), lambda i: (i, 0, 0, 0)),
            row(MLA_ROW_W), row(MLA_ROW_W),
        ],
        out_shape=[
            jax.ShapeDtypeStruct((n // tm, MLA_HEADS, tm, MLA_ROW_W), BF16),
            jax.ShapeDtypeStruct((n, MLA_ROW_W), F32),
            jax.ShapeDtypeStruct((n, MLA_ROW_W), BF16),
        ],
        compiler_params=_cparams(("parallel",)),
        name="mla_proj",
    )(zq, zkv, _vec3(qn), _vec3(kvn), wuq, wuk, *tabs)


def _masked(x, mask, heads, fill):
    if mask is None:
        return x
    tq, tk = mask.shape
    return jnp.where(mask[None], x.reshape(heads, tq, tk), fill).reshape(heads * tq, tk)


def _softmax_step(s, mask, heads, m_ref, l_ref, acc_ref, v):
    s = _masked(s, mask, heads, NEG)
    m_old = m_ref[...]
    m_new = jnp.maximum(m_old, jnp.max(s, -1, keepdims=True))
    alpha = jnp.exp(m_old - m_new)
    p = _masked(jnp.exp(s - m_new), mask, heads, 0.0)
    l_ref[...] = alpha * l_ref[...] + jnp.sum(p, -1, keepdims=True)
    acc_ref[...] = alpha * acc_ref[...] + _dot(p, v)
    m_ref[...] = m_new


def _softmax_once(s, mask, heads, v):
    s = _masked(s, mask, heads, NEG)
    e = _masked(jnp.exp(s - jnp.max(s, -1, keepdims=True)), mask, heads, 0.0)
    den = jnp.sum(e, -1, keepdims=True)
    p = e / jnp.where(den > 0, den, 1.0)
    return _dot(p, v), p


def _softmax_init(m_ref, l_ref, acc_ref):
    m_ref[...] = jnp.full(m_ref.shape, NEG, F32)
    l_ref[...] = jnp.zeros(l_ref.shape, F32)
    acc_ref[...] = jnp.zeros(acc_ref.shape, F32)


def _softmax_out(l_ref, acc_ref):
    l = l_ref[...]
    return acc_ref[...] / jnp.where(l > 0, l, 1.0)


def _mla_prompt_body(q_ref, kv_ref, wuv_ref, o_ref, m_ref, l_ref, acc_ref, *, tk):
    i = pl.program_id(1)
    tq = q_ref.shape[2]
    q = q_ref[0].reshape(MLA_HEADS * tq, MLA_ROW_W)
    _softmax_init(m_ref, l_ref, acc_ref)
    qpos = i * tq + lax.broadcasted_iota(jnp.int32, (tq, tk), 0)
    koff = lax.broadcasted_iota(jnp.int32, (tq, tk), 1)

    def step(j, carry):
        k = kv_ref[0, pl.ds(pl.multiple_of(j * tk, tk), tk), :]
        _softmax_step(_dot_nt(q, k), j * tk + koff <= qpos, MLA_HEADS, m_ref, l_ref, acc_ref,
                      k[:, :MLA_KV_LORA])
        return carry

    lax.fori_loop(0, ((i + 1) * tq + tk - 1) // tk, step, 0)
    o_lat = _softmax_out(l_ref, acc_ref)
    for h in range(MLA_HEADS):
        o_ref[:, h * MLA_V:(h + 1) * MLA_V] = _dot(o_lat[h * tq:(h + 1) * tq], wuv_ref[h])


def _mla_prompt(qcat, rows_b, wuv, l, b, s):
    tq = Q_TILE
    nq = s // tq
    tk = _pick_tile(s, (256, 128))
    return pl.pallas_call(
        functools.partial(_mla_prompt_body, tk=tk),
        grid=(b, nq),
        in_specs=[
            pl.BlockSpec((1, MLA_HEADS, tq, MLA_ROW_W), lambda bi, i: (bi * nq + i, 0, 0, 0)),
            pl.BlockSpec((1, s, MLA_ROW_W), lambda bi, i: (bi, 0, 0)),
            pl.BlockSpec((None, MLA_HEADS, MLA_KV_LORA, MLA_V), lambda bi, i: (l, 0, 0, 0)),
        ],
        out_specs=pl.BlockSpec((tq, BRANCH_W), lambda bi, i: (bi * nq + i, 0)),
        out_shape=jax.ShapeDtypeStruct((b * s, BRANCH_W), F32),
        scratch_shapes=[pltpu.VMEM((MLA_HEADS * tq, 1), F32), pltpu.VMEM((MLA_HEADS * tq, 1), F32),
                        pltpu.VMEM((MLA_HEADS * tq, MLA_KV_LORA), F32)],
        compiler_params=_cparams(("parallel", "arbitrary")),
        name="mla_prompt",
    )(qcat, rows_b.reshape(b, s, MLA_ROW_W), wuv)


def _mla_sample_body(pt_ref, q_ref, *refs, pc, t_new):
    page_refs = refs[:pc]
    new_ref, wuv_ref, o_ref, m_ref, l_ref, acc_ref = refs[pc:]
    c = pl.program_id(1)

    @pl.when(c == 0)
    def _():
        _softmax_init(m_ref, l_ref, acc_ref)

    q = q_ref[0][:, :MLA_CACHE_W]
    keys = jnp.concatenate([r[...].astype(BF16) for r in page_refs], 0)
    _softmax_step(_dot_nt(q, keys), None, MLA_HEADS, m_ref, l_ref, acc_ref, keys[:, :MLA_KV_LORA])

    @pl.when(c == pl.num_programs(1) - 1)
    def _():
        k_new = new_ref[0][:, :MLA_CACHE_W]
        tq = lax.broadcasted_iota(jnp.int32, (t_new, t_new), 0)
        tk = lax.broadcasted_iota(jnp.int32, (t_new, t_new), 1)
        _softmax_step(_dot_nt(q, k_new), tk <= tq, MLA_HEADS, m_ref, l_ref, acc_ref, k_new[:, :MLA_KV_LORA])
        o_lat = _softmax_out(l_ref, acc_ref)
        for h in range(MLA_HEADS):
            o_ref[0, :, h * MLA_V:(h + 1) * MLA_V] = _dot(o_lat[h * t_new:(h + 1) * t_new], wuv_ref[h])


def _page_chunk(n_pages):
    return _pick_tile(n_pages, (16, 8, 4, 2, 1))


def _mla_sample(page_table, q, cache, rows_new, wuv, l):
    db, rows, _ = q.shape
    t_new = rows // MLA_HEADS
    n_pages = page_table.shape[1]
    page = cache.shape[2]
    pc = _page_chunk(n_pages)

    def page_spec(k):
        return pl.BlockSpec((None, None, page, MLA_CACHE_W),
                            lambda bi, c, pt: (l, pt[bi, c * pc + k], 0, 0))

    grid_spec = pltpu.PrefetchScalarGridSpec(
        num_scalar_prefetch=1,
        grid=(db, n_pages // pc),
        in_specs=[pl.BlockSpec((1, rows, MLA_ROW_W), lambda bi, c, pt: (bi, 0, 0))]
        + [page_spec(k) for k in range(pc)]
        + [pl.BlockSpec((1, t_new, MLA_ROW_W), lambda bi, c, pt: (bi, 0, 0)),
           pl.BlockSpec((None, MLA_HEADS, MLA_KV_LORA, MLA_V), lambda bi, c, pt: (l, 0, 0, 0))],
        out_specs=pl.BlockSpec((1, t_new, BRANCH_W), lambda bi, c, pt: (bi, 0, 0)),
        scratch_shapes=[pltpu.VMEM((rows, 1), F32), pltpu.VMEM((rows, 1), F32),
                        pltpu.VMEM((rows, MLA_KV_LORA), F32)],
    )
    return pl.pallas_call(
        functools.partial(_mla_sample_body, pc=pc, t_new=t_new),
        grid_spec=grid_spec,
        out_shape=jax.ShapeDtypeStruct((db, t_new, BRANCH_W), F32),
        compiler_params=_cparams(("parallel", "arbitrary")),
        name="mla_sample",
    )(page_table, q, *([cache] * pc), rows_new, wuv)


def _nsa_proj_body(zq_ref, zkv_ref, zg_ref, cq_ref, sq_ref, ckv_ref, skv_ref, q_ref, kv_ref, g_ref):
    q = _rope_apply(zq_ref[...], cq_ref[...], sq_ref[...]) * NSA_SCALE
    for h in range(NSA_HEADS):
        q_ref[0, h] = q[:, h * NSA_HD:(h + 1) * NSA_HD].astype(BF16)
    kv_ref[...] = _rope_apply(zkv_ref[...], ckv_ref[...], skv_ref[...])
    g_ref[...] = jax.nn.sigmoid(zg_ref[...])


def _nsa_proj(zq, zkv, zg, tabs):
    n = zq.shape[0]
    tm = Q_TILE
    kvw = zkv.shape[1]
    row = lambda w: pl.BlockSpec((tm, w), lambda i: (i, 0))
    return pl.pallas_call(
        _nsa_proj_body,
        grid=(n // tm,),
        in_specs=[row(BRANCH_W), row(kvw), row(LANE), row(BRANCH_W), row(BRANCH_W), row(kvw), row(kvw)],
        out_specs=[pl.BlockSpec((1, NSA_HEADS, tm, NSA_HD), lambda i: (i, 0, 0, 0)), row(kvw), row(LANE)],
        out_shape=[
            jax.ShapeDtypeStruct((n // tm, NSA_HEADS, tm, NSA_HD), BF16),
            jax.ShapeDtypeStruct((n, kvw), F32),
            jax.ShapeDtypeStruct((n, LANE), F32),
        ],
        compiler_params=_cparams(("parallel",)),
        name="nsa_proj",
    )(zq, zkv, zg, *tabs)


def _bmean_prompt_body(kv_ref, o_ref):
    nb = o_ref.shape[1]
    x = kv_ref[0][:nb * NSA_BLOCK]
    o_ref[0] = jnp.sum(x.reshape(nb, NSA_BLOCK, LANE), 1) * (1.0 / NSA_BLOCK)


def _bmean_prompt(kv, b, s):
    nb = s // NSA_BLOCK
    return pl.pallas_call(
        _bmean_prompt_body,
        grid=(b,),
        in_specs=[pl.BlockSpec((1, s, LANE), lambda bi: (bi, 0, 0))],
        out_specs=pl.BlockSpec((1, nb, LANE), lambda bi: (bi, 0, 0)),
        out_shape=jax.ShapeDtypeStruct((b, nb, LANE), F32),
        compiler_params=_cparams(("parallel",)),
        name="bmean_prompt",
    )(kv.reshape(b, s, kv.shape[-1]))


def _bmean_sample_body(pt_ref, *refs, pc):
    o_ref = refs[pc]
    page = refs[0].shape[0]
    per = page // NSA_BLOCK
    for k in range(pc):
        x = refs[k][...]
        o_ref[0, k * per:(k + 1) * per, :] = jnp.sum(x.reshape(per, NSA_BLOCK, LANE), 1) * (1.0 / NSA_BLOCK)


def _bmean_sample(page_table, cache, l):
    db, n_pages = page_table.shape
    page = cache.shape[2]
    pc = _page_chunk(n_pages)
    per = page // NSA_BLOCK

    def page_spec(k):
        return pl.BlockSpec((None, None, page, LANE), lambda bi, c, pt: (l, pt[bi, c * pc + k], 0, 0))

    grid_spec = pltpu.PrefetchScalarGridSpec(
        num_scalar_prefetch=1,
        grid=(db, n_pages // pc),
        in_specs=[page_spec(k) for k in range(pc)],
        out_specs=pl.BlockSpec((1, pc * per, LANE), lambda bi, c, pt: (bi, c, 0)),
    )
    return pl.pallas_call(
        functools.partial(_bmean_sample_body, pc=pc),
        grid_spec=grid_spec,
        out_shape=jax.ShapeDtypeStruct((db, n_pages * per, LANE), F32),
        compiler_params=_cparams(("parallel", "arbitrary")),
        name="bmean_sample",
    )(page_table, *([cache] * pc))


def _cmp_proj_body(x_ref, phi_ref, c_ref, s_ref, o_ref):
    bt, nb, _ = x_ref.shape
    y = _dot(x_ref[...].reshape(bt * nb, LANE), phi_ref[...]).reshape(bt, nb, LANE)
    o_ref[...] = _rope_apply(y, c_ref[...][None], s_ref[...][None])


def _cmp_proj(means, phi, cos, sin, l):
    b, nb, _ = means.shape
    bt = _pick_tile(b, (8, 4, 2, 1))
    return pl.pallas_call(
        _cmp_proj_body,
        grid=(b // bt,),
        in_specs=[
            pl.BlockSpec((bt, nb, LANE), lambda i: (i, 0, 0)),
            pl.BlockSpec((None, LANE, LANE), lambda i: (l, 0, 0)),
            pl.BlockSpec((nb, LANE), lambda i: (0, 0)),
            pl.BlockSpec((nb, LANE), lambda i: (0, 0)),
        ],
        out_specs=pl.BlockSpec((bt, nb, LANE), lambda i: (i, 0, 0)),
        out_shape=jax.ShapeDtypeStruct((b, nb, LANE), F32),
        compiler_params=_cparams(("parallel",)),
        name="cmp_proj",
    )(means, phi, cos, sin)


def _cmp_attend(q, kbvb, vis, heads):
    tq, nb = vis.shape
    o, p = _softmax_once(_dot_nt(q, kbvb[:, :NSA_HD]), vis, heads, kbvb[:, NSA_HD:])
    return o, jnp.sum(p.reshape(heads, tq, nb), 0)


def _topk_select(imp, cand, k):
    nb = imp.shape[1]
    lane = lax.broadcasted_iota(jnp.int32, imp.shape, 1).astype(F32)
    v = jnp.where(cand, imp, -1.0)
    sel = jnp.zeros(imp.shape, F32)
    for _ in range(k):
        m = jnp.max(v, -1, keepdims=True)
        first = jnp.min(jnp.where(v == m, lane, float(nb)), -1, keepdims=True)
        hit = lane == first
        sel = jnp.where(hit & (m >= 0.0), 1.0, sel)
        v = jnp.where(hit, -2.0, v)
    return sel


def _nsa_prompt_body(q_ref, kbvb_ref, kv_ref, g_ref, o_ref, m_ref, l_ref, acc_ref):
    i = pl.program_id(1)
    tq = q_ref.shape[2]
    nb = kbvb_ref.shape[1]
    heads = NSA_HEADS
    q = q_ref[0].reshape(heads * tq, NSA_HD)
    qpos_b = i * tq + lax.broadcasted_iota(jnp.int32, (tq, nb), 0)
    blk = lax.broadcasted_iota(jnp.int32, (tq, nb), 1)
    o_cmp, imp = _cmp_attend(q, kbvb_ref[0], blk * NSA_BLOCK + (NSA_BLOCK - 1) <= qpos_b, heads)
    cur = qpos_b // NSA_BLOCK
    sel = jnp.where(blk == cur, 1.0, _topk_select(imp, blk < cur, min(NSA_TOPN - 1, nb)))

    qpos = i * tq + lax.broadcasted_iota(jnp.int32, (tq, tq), 0)
    koff = lax.broadcasted_iota(jnp.int32, (tq, tq), 1)
    per = tq // NSA_BLOCK
    e_row = lax.broadcasted_iota(jnp.int32, (nb, tq), 0)
    e_col = lax.broadcasted_iota(jnp.int32, (nb, tq), 1) // NSA_BLOCK

    def sel_step(j, carry):
        kv = kv_ref[0, pl.ds(pl.multiple_of(j * tq, tq), tq), 2 * NSA_HD:4 * NSA_HD]
        expand = jnp.where(e_row == j * per + e_col, 1.0, 0.0)
        chosen = _dot(sel, expand) > 0.5
        mask = chosen & (j * tq + koff <= qpos)
        _softmax_step(_dot_nt(q, kv[:, :NSA_HD]), mask, heads, m_ref, l_ref, acc_ref, kv[:, NSA_HD:])
        return carry

    _softmax_init(m_ref, l_ref, acc_ref)
    lax.fori_loop(0, i + 1, sel_step, 0)
    o_sel = _softmax_out(l_ref, acc_ref)

    def win_step(j, carry):
        kv = kv_ref[0, pl.ds(pl.multiple_of(j * tq, tq), tq), 4 * NSA_HD:6 * NSA_HD]
        rel = qpos - (j * tq + koff)
        mask = (rel >= 0) & (rel < NSA_WINDOW)
        _softmax_step(_dot_nt(q, kv[:, :NSA_HD]), mask, heads, m_ref, l_ref, acc_ref, kv[:, NSA_HD:])
        return carry

    _softmax_init(m_ref, l_ref, acc_ref)
    lax.fori_loop(jnp.maximum(i - NSA_WINDOW // tq, 0), i + 1, win_step, 0)
    o_win = _softmax_out(l_ref, acc_ref)

    g = g_ref[...]
    for h in range(heads):
        rows = slice(h * tq, (h + 1) * tq)
        o_ref[:, h * NSA_HD:(h + 1) * NSA_HD] = (
            g[:, h:h + 1] * o_cmp[rows] + g[:, heads + h:heads + h + 1] * o_sel[rows]
            + g[:, 2 * heads + h:2 * heads + h + 1] * o_win[rows])


def _nsa_prompt(q, kbvb, kv, gates, b, s):
    tq = Q_TILE
    nq = s // tq
    nb = kbvb.shape[1]
    kvw = kv.shape[-1]
    rows = NSA_HEADS * tq
    return pl.pallas_call(
        _nsa_prompt_body,
        grid=(b, nq),
        in_specs=[
            pl.BlockSpec((1, NSA_HEADS, tq, NSA_HD), lambda bi, i: (bi * nq + i, 0, 0, 0)),
            pl.BlockSpec((1, nb, LANE), lambda bi, i: (bi, 0, 0)),
            pl.BlockSpec((1, s, kvw), lambda bi, i: (bi, 0, 0)),
            pl.BlockSpec((tq, LANE), lambda bi, i: (bi * nq + i, 0)),
        ],
        out_specs=pl.BlockSpec((tq, BRANCH_W), lambda bi, i: (bi * nq + i, 0)),
        out_shape=jax.ShapeDtypeStruct((b * s, BRANCH_W), F32),
        scratch_shapes=[pltpu.VMEM((rows, 1), F32), pltpu.VMEM((rows, 1), F32), pltpu.VMEM((rows, NSA_HD), F32)],
        compiler_params=_cparams(("parallel", "arbitrary")),
        name="nsa_prompt",
    )(q, kbvb, kv.reshape(b, s, kvw), gates)


def _nsa_cmp_sample_body(q_ref, kbvb_ref, o_ref, sel_ref, *, t_new):
    nb = kbvb_ref.shape[1]
    vis = jnp.full((t_new, nb), True)
    o_cmp, imp = _cmp_attend(q_ref[0], kbvb_ref[0], vis, NSA_HEADS)
    o_ref[0] = o_cmp
    sel_ref[0] = _topk_select(imp, vis, min(NSA_TOPN - 1, nb))


def _nsa_cmp_sample(q, kbvb, t_new):
    db, rows, _ = q.shape
    nb = kbvb.shape[1]
    return pl.pallas_call(
        functools.partial(_nsa_cmp_sample_body, t_new=t_new),
        grid=(db,),
        in_specs=[pl.BlockSpec((1, rows, NSA_HD), lambda bi: (bi, 0, 0)),
                  pl.BlockSpec((1, nb, LANE), lambda bi: (bi, 0, 0))],
        out_specs=[pl.BlockSpec((1, rows, NSA_HD), lambda bi: (bi, 0, 0)),
                   pl.BlockSpec((1, t_new, nb), lambda bi: (bi, 0, 0))],
        out_shape=[jax.ShapeDtypeStruct((db, rows, NSA_HD), F32), jax.ShapeDtypeStruct((db, t_new, nb), F32)],
        compiler_params=_cparams(("parallel",)),
        name="nsa_cmp_sample",
    )(q, kbvb)


def _nsa_sel_sample_body(pt_ref, q_ref, sel_ref, exp_ref, *refs, pc, t_new):
    page_refs = refs[:pc]
    new_ref, o_ref, m_ref, l_ref, acc_ref = refs[pc:]
    c = pl.program_id(1)

    @pl.when(c == 0)
    def _():
        _softmax_init(m_ref, l_ref, acc_ref)

    q = q_ref[0]
    kv = jnp.concatenate([r[...].astype(BF16) for r in page_refs], 0)
    chosen = _dot(sel_ref[0, 0], exp_ref[...]) > 0.5
    _softmax_step(_dot_nt(q, kv[:, :NSA_HD]), chosen, NSA_HEADS, m_ref, l_ref, acc_ref, kv[:, NSA_HD:])

    @pl.when(c == pl.num_programs(1) - 1)
    def _():
        kv_new = new_ref[0]
        tq = lax.broadcasted_iota(jnp.int32, (t_new, t_new), 0)
        tk = lax.broadcasted_iota(jnp.int32, (t_new, t_new), 1)
        _softmax_step(_dot_nt(q, kv_new[:, :NSA_HD]), tk <= tq, NSA_HEADS, m_ref, l_ref, acc_ref,
                      kv_new[:, NSA_HD:])
        o_ref[0] = _softmax_out(l_ref, acc_ref)


def _nsa_sel_sample(page_table, q, sel, cache, kv_new, l, t_new):
    db, rows, _ = q.shape
    n_pages = page_table.shape[1]
    page = cache.shape[2]
    pc = _page_chunk(n_pages)
    per = page // NSA_BLOCK
    nch = n_pages // pc
    sel_c = sel.reshape(db, t_new, nch, pc * per).transpose(0, 2, 1, 3)
    expand = (jnp.arange(pc * per)[:, None] == jnp.arange(pc * page)[None, :] // NSA_BLOCK).astype(BF16)

    def page_spec(k):
        return pl.BlockSpec((None, None, page, LANE), lambda bi, c, pt: (l, pt[bi, c * pc + k], 0, 1))

    grid_spec = pltpu.PrefetchScalarGridSpec(
        num_scalar_prefetch=1,
        grid=(db, nch),
        in_specs=[pl.BlockSpec((1, rows, NSA_HD), lambda bi, c, pt: (bi, 0, 0)),
                  pl.BlockSpec((1, 1, t_new, pc * per), lambda bi, c, pt: (bi, c, 0, 0)),
                  pl.BlockSpec((pc * per, pc * page), lambda bi, c, pt: (0, 0))]
        + [page_spec(k) for k in range(pc)]
        + [pl.BlockSpec((1, t_new, LANE), lambda bi, c, pt: (bi, 0, 0))],
        out_specs=pl.BlockSpec((1, rows, NSA_HD), lambda bi, c, pt: (bi, 0, 0)),
        scratch_shapes=[pltpu.VMEM((rows, 1), F32), pltpu.VMEM((rows, 1), F32), pltpu.VMEM((rows, NSA_HD), F32)],
    )
    return pl.pallas_call(
        functools.partial(_nsa_sel_sample_body, pc=pc, t_new=t_new),
        grid_spec=grid_spec,
        out_shape=jax.ShapeDtypeStruct((db, rows, NSA_HD), F32),
        compiler_params=_cparams(("parallel", "arbitrary")),
        name="nsa_sel_sample",
    )(page_table, q, sel_c, expand, *([cache] * pc), kv_new)


def _nsa_win_sample_body(q_ref, buf_ref, new_ref, ocmp_ref, osel_ref, g_ref, o_ref, *, t_new):
    heads = NSA_HEADS
    lw = buf_ref.shape[1]
    q = q_ref[0]
    kv = jnp.concatenate([buf_ref[0], new_ref[0]], 0)
    tq = lax.broadcasted_iota(jnp.int32, (t_new, lw + t_new), 0)
    rel = tq + lw - lax.broadcasted_iota(jnp.int32, (t_new, lw + t_new), 1)
    o_win, _ = _softmax_once(_dot_nt(q, kv[:, :NSA_HD]), (rel >= 0) & (rel < NSA_WINDOW), heads, kv[:, NSA_HD:])
    o_cmp = ocmp_ref[0]
    o_sel = osel_ref[0]
    g = g_ref[0]
    for h in range(heads):
        rows = slice(h * t_new, (h + 1) * t_new)
        o_ref[0, :, h * NSA_HD:(h + 1) * NSA_HD] = (
            g[:, h:h + 1] * o_cmp[rows] + g[:, heads + h:heads + h + 1] * o_sel[rows]
            + g[:, 2 * heads + h:2 * heads + h + 1] * o_win[rows])


def _nsa_win_sample(q, buf, kv_new, o_cmp, o_sel, gates, t_new):
    db, rows, _ = q.shape
    lw = buf.shape[1]
    blk = lambda r, w: pl.BlockSpec((1, r, w), lambda bi: (bi, 0, 0))
    return pl.pallas_call(
        functools.partial(_nsa_win_sample_body, t_new=t_new),
        grid=(db,),
        in_specs=[blk(rows, NSA_HD), blk(lw, LANE), blk(t_new, LANE), blk(rows, NSA_HD), blk(rows, NSA_HD),
                  blk(t_new, LANE)],
        out_specs=blk(t_new, BRANCH_W),
        out_shape=jax.ShapeDtypeStruct((db, t_new, BRANCH_W), F32),
        compiler_params=_cparams(("parallel",)),
        name="nsa_win_sample",
    )(q, buf, kv_new, o_cmp, o_sel, gates)


def _rwkv_pre_body(zr_ref, zp_ref, mu_ref, w0_ref, a0_ref, wa_ref, g2_ref, kk_ref, ka_ref, ones_ref,
                   r_ref, k_ref, v_ref, lw_ref, kkn_ref, b_ref, g_ref):
    c = BRANCH_W
    zr = zr_ref[...]
    zs = zr + (zp_ref[...] - zr) * mu_ref[...]
    r, k, v = zs[:, :c], zs[:, c:2 * c], zs[:, 2 * c:3 * c]
    lora = zs[:, 3 * c:3 * c + LANE]
    lane = lax.broadcasted_iota(jnp.int32, lora.shape, 1)
    wa = _dot(jnp.where(lane < RWKV_W_LORA, jnp.tanh(lora), lora), wa_ref[...])
    x = -(w0_ref[...] + wa[:, :c])
    softplus = jnp.maximum(x, 0.0) + jnp.log(1.0 + jnp.exp(-jnp.abs(x)))
    a = jax.nn.sigmoid(a0_ref[...] + wa[:, c:])
    kk = k * kk_ref[...]
    norm = jnp.sqrt(_dot_split(kk * kk, ones_ref[...]))
    kk = kk / jnp.maximum(norm, 1e-12)
    r_ref[...] = r
    k_ref[...] = k * (1.0 + (a - 1.0) * ka_ref[...])
    v_ref[...] = v
    lw_ref[...] = -jnp.exp(-softplus - 0.5)
    kkn_ref[...] = kk
    b_ref[...] = kk * a
    g_ref[...] = _dot(jax.nn.sigmoid(zs[:, 3 * c + LANE:]), g2_ref[...])


def _rwkv_pre(zr, zprev, mu, w0, a0, wa, g2, k_k, k_a, ones_bd, l):
    n = zr.shape[0]
    tm = _pick_tile(n, (256, 128))
    c = BRANCH_W
    row = lambda w: pl.BlockSpec((tm, w), lambda i: (i, 0))
    vec = lambda w: _vec_spec(w, l, 1)
    return pl.pallas_call(
        _rwkv_pre_body,
        grid=(n // tm,),
        in_specs=[row(RWKV_IN), row(RWKV_IN), vec(RWKV_IN), vec(c), vec(c),
                  pl.BlockSpec((None, LANE, 2 * c), lambda i: (l, 0, 0)),
                  pl.BlockSpec((None, RWKV_G_LORA, c), lambda i: (l, 0, 0)),
                  vec(c), vec(c), pl.BlockSpec((c, c), lambda i: (0, 0))],
        out_specs=[row(c)] * 7,
        out_shape=[jax.ShapeDtypeStruct((n, c), F32)] * 7,
        compiler_params=_cparams(("parallel",)),
        name="rwkv_pre",
    )(zr, zprev, _vec3(mu), _vec3(w0), _vec3(a0), wa, g2, _vec3(k_k), _vec3(k_a), ones_bd)


def _rwkv_scan_body(r_ref, k_ref, v_ref, lw_ref, kk_ref, b_ref, s0_ref, y_ref, sf_ref, st_ref, *, c_len, nh):
    ci = pl.program_id(1)
    ch = nh * RWKV_HS
    rows = nh * c_len

    @pl.when(ci == 0)
    def _():
        st_ref[...] = s0_ref[0]

    lw = lw_ref[0]
    tri = (lax.broadcasted_iota(jnp.int32, (c_len, c_len), 0)
           >= lax.broadcasted_iota(jnp.int32, (c_len, c_len), 1)).astype(BF16)
    lp = None
    rem = lw
    for _ in range(3):
        piece = rem.astype(BF16)
        d = jnp.dot(tri, piece, preferred_element_type=F32)
        lp = d if lp is None else lp + d
        rem = rem - piece.astype(F32)
    p = jnp.exp(lp)
    p_inv = jnp.exp(-lp)
    hm = (lax.broadcasted_iota(jnp.int32, (rows, ch), 0) // c_len
          == lax.broadcasted_iota(jnp.int32, (rows, ch), 1) // RWKV_HS)

    def stack(a):
        return jnp.where(hm, jnp.concatenate([a] * nh, 0), 0.0).astype(BF16)

    x_kk = stack(kk_ref[0] * jnp.exp(lp - lw))
    x_r = stack(r_ref[0] * p)
    y_k = stack(k_ref[0] * p_inv)
    y_b = stack(b_ref[0] * p_inv)
    v = v_ref[0]
    vs = jnp.concatenate([v[:, h * RWKV_HS:(h + 1) * RWKV_HS] for h in range(nh)], 0)

    ti = lax.broadcasted_iota(jnp.int32, (rows, rows), 0) % c_len
    tj = lax.broadcasted_iota(jnp.int32, (rows, rows), 1) % c_len
    strict, incl = ti > tj, ti >= tj
    n_mat = jnp.where(strict, -_dot_nt(x_kk, y_b), 0.0)
    a_kk = jnp.where(strict, _dot_nt(x_kk, y_k), 0.0)
    a_rk = jnp.where(incl, _dot_nt(x_r, y_k), 0.0)
    a_rb = jnp.where(incl, _dot_nt(x_r, y_b), 0.0)
    eye = jnp.where(lax.broadcasted_iota(jnp.int32, (rows, rows), 0)
                    == lax.broadcasted_iota(jnp.int32, (rows, rows), 1), 1.0, 0.0)
    t_inv = eye + n_mat
    pw = n_mat
    span = 2
    while span < c_len:
        pw = _dot(pw, pw)
        t_inv = t_inv + _dot(t_inv, pw)
        span *= 2

    st = st_ref[...]
    u = _dot(t_inv, _dot_nt(x_kk, st) + _dot(a_kk, vs))
    y = _dot_nt(x_r, st) + _dot(a_rk, vs) - _dot(a_rb, u)
    st_new = p[c_len - 1:c_len, :] * (st + _dot(vs.T, y_k) - _dot(u.T, y_b))
    st_ref[...] = st_new
    for h in range(nh):
        y_ref[0, :, h * RWKV_HS:(h + 1) * RWKV_HS] = y[h * c_len:(h + 1) * c_len]

    @pl.when(ci == pl.num_programs(1) - 1)
    def _():
        sf_ref[0] = st_new


def _rwkv_scan(r, k, v, lw, kk, b, s0, c_len):
    nseq, t, ch = r.shape
    nh = ch // RWKV_HS
    assert nh * c_len == SCAN_ROWS
    blk = pl.BlockSpec((1, c_len, ch), lambda si, ci: (si, ci, 0))
    st = pl.BlockSpec((1, RWKV_HS, ch), lambda si, ci: (si, 0, 0))
    return pl.pallas_call(
        functools.partial(_rwkv_scan_body, c_len=c_len, nh=nh),
        grid=(nseq, t // c_len),
        in_specs=[blk] * 6 + [st],
        out_specs=[blk, st],
        out_shape=[jax.ShapeDtypeStruct((nseq, t, ch), F32), jax.ShapeDtypeStruct((nseq, RWKV_HS, ch), F32)],
        scratch_shapes=[pltpu.VMEM((RWKV_HS, ch), F32)],
        compiler_params=_cparams(("parallel", "arbitrary")),
        name="rwkv_scan",
    )(r, k, v, lw, kk, b, s0)


def _rwkv_post_body(y_ref, r_ref, k_ref, v_ref, g_ref, rk_ref, lnw_ref, lnb_ref, ones_ref, o_ref):
    y = y_ref[...]
    ones = ones_ref[...]
    mu = _dot_split(y, ones) * (1.0 / RWKV_HS)
    dev = y - mu
    var = _dot_split(dev * dev, ones) * (1.0 / RWKV_HS)
    yn = dev * lax.rsqrt(var + RWKV_GN_EPS) * lnw_ref[...] + lnb_ref[...]
    bonus = _dot_split(r_ref[...] * k_ref[...] * rk_ref[...], ones) * v_ref[...]
    o_ref[...] = (yn + bonus) * g_ref[...]


def _rwkv_post(y, r, k, v, g, r_k, ln_w, ln_b, ones_bd, l):
    n, c = y.shape
    tm = _pick_tile(n, (256, 128))
    row = pl.BlockSpec((tm, c), lambda i: (i, 0))
    vec = _vec_spec(c, l, 1)
    return pl.pallas_call(
        _rwkv_post_body,
        grid=(n // tm,),
        in_specs=[row] * 5 + [vec] * 3 + [pl.BlockSpec((c, c), lambda i: (0, 0))],
        out_specs=row,
        out_shape=jax.ShapeDtypeStruct((n, c), F32),
        compiler_params=_cparams(("parallel",)),
        name="rwkv_post",
    )(y, r, k, v, g, _vec3(r_k), _vec3(ln_w), _vec3(ln_b), ones_bd)


def _rope_tables(pos):
    inv = ROPE_THETA ** (-jnp.arange(0, NSA_HD, 2, dtype=F32) / NSA_HD)
    ang = pos.astype(F32)[:, None] * inv[None, :]
    c, s = jnp.cos(ang), jnp.sin(ang)
    return jnp.concatenate([c, c], -1), jnp.concatenate([-s, s], -1)


def _table(cos, sin, groups):
    n = cos.shape[0]
    cs = [cos if rot else jnp.ones((n, w), F32) for w, rot in groups]
    ss = [sin if rot else jnp.zeros((n, w), F32) for w, rot in groups]
    return jnp.concatenate(cs, -1), jnp.concatenate(ss, -1)


def _heads_first(x, tile, group, t_new):
    nt, h, _, d = x.shape
    x = x.reshape(nt, h, tile // t_new, t_new, d).transpose(0, 2, 1, 3, 4)
    return x.reshape(group, h * t_new, d)


def kernel(x_prompt, x_sample, cache_mla, cache_nsa, state_nsa_win, state_pool, state_rwkv, state_rwkv_shift,
           page_table, norm_ffn1, ffn1_w_gate, ffn1_w_up, ffn1_w_down, norm_mix, w_in, w_branch, w_out, pool_w,
           pool_scale, mla_q_norm, mla_w_uq, mla_kv_norm, mla_w_uk, mla_w_uv, nsa_phi_k, nsa_phi_v, rwkv_mu,
           rwkv_w0, rwkv_w2, rwkv_a0, rwkv_a2, rwkv_g2, rwkv_k_k, rwkv_k_a, rwkv_r_k, rwkv_ln_w, rwkv_ln_b,
           norm_ffn2, ffn2_w_gate, ffn2_w_up, ffn2_w_down, norm_final):
    b, s, d = x_prompt.shape
    db, t_new = x_sample.shape[:2]
    depth = norm_ffn1.shape[0]
    n_p, n_s = b * s, db * t_new
    n_pages, page = page_table.shape[1], cache_mla.shape[2]
    past = n_pages * page
    win_len = state_nsa_win.shape[2]
    n_phys = cache_mla.shape[1]
    assert d == D_MODEL and s % Q_TILE == 0 and n_s % Q_TILE == 0 and Q_TILE % t_new == 0
    assert s >= win_len and page % NSA_BLOCK == 0 and t_new < NSA_BLOCK and t_new <= POOL_HALO
    samp_c = t_new
    samp_grp = SCAN_ROWS // (samp_c * RWKV_HEADS)
    assert db % samp_grp == 0 and s % 32 == 0

    bf = lambda w: w.astype(BF16)
    f1 = (bf(ffn1_w_gate), bf(ffn1_w_up), bf(ffn1_w_down))
    f2 = (bf(ffn2_w_gate), bf(ffn2_w_up), bf(ffn2_w_down))
    o_mla = BRANCH_W
    o_nsa = o_mla + MLA_Q_LORA + MLA_CACHE_W
    o_rw = o_nsa + BRANCH_W + 6 * NSA_HD + 3 * NSA_HEADS
    o_gate = o_rw + RWKV_IN
    zpad = lambda w: jnp.zeros((depth, d, w), w_in.dtype)
    w_in_p = bf(jnp.concatenate([
        w_in[:, :, o_gate:], w_in[:, :, :o_mla], w_in[:, :, o_mla:o_mla + MLA_Q_LORA],
        w_in[:, :, o_mla + MLA_Q_LORA:o_nsa], zpad(MLA_ROW_W - MLA_CACHE_W),
        w_in[:, :, o_nsa:o_nsa + BRANCH_W], w_in[:, :, o_nsa + BRANCH_W:o_nsa + BRANCH_W + 6 * NSA_HD],
        w_in[:, :, o_nsa + BRANCH_W + 6 * NSA_HD:o_rw], zpad(LANE - 3 * NSA_HEADS),
        w_in[:, :, o_rw:o_gate], zpad(Z_W - Z_END)], -1))
    wb = bf(w_branch)
    wo = bf(w_out)
    pw = bf(pool_w)
    uq = mla_w_uq.reshape(depth, MLA_Q_LORA, MLA_HEADS, MLA_NOPE + MLA_ROPE)
    wuq = bf(jnp.concatenate([uq[..., :MLA_NOPE].reshape(depth, MLA_Q_LORA, -1),
                              uq[..., MLA_NOPE:].reshape(depth, MLA_Q_LORA, -1)], -1))
    wuk = bf(mla_w_uk.transpose(0, 2, 3, 1))
    wuv = bf(mla_w_uv.transpose(0, 2, 1, 3))
    zz = jnp.zeros((depth, NSA_HD, NSA_HD), F32)
    phi = bf(jnp.concatenate([jnp.concatenate([nsa_phi_k, zz], -1), jnp.concatenate([zz, nsa_phi_v], -1)], 1))
    zw = jnp.zeros((depth, RWKV_W_LORA, BRANCH_W), F32)
    wa = bf(jnp.concatenate([jnp.concatenate([rwkv_w2, zw], -1), jnp.concatenate([zw, rwkv_a2], -1)], 1))
    g2 = bf(rwkv_g2)
    r_k = rwkv_r_k.reshape(depth, BRANCH_W)
    ones_bd = (jnp.arange(BRANCH_W)[:, None] // RWKV_HS == jnp.arange(BRANCH_W)[None, :] // RWKV_HS).astype(BF16)

    pos = jnp.concatenate([jnp.tile(jnp.arange(s), b), jnp.tile(past + jnp.arange(t_new), db)])
    cos, sin = _rope_tables(pos)
    tab_mq = _table(cos, sin, [(MLA_HEADS * MLA_NOPE, False)] + [(MLA_ROPE, True)] * MLA_HEADS)
    tab_mkv = _table(cos, sin, [(MLA_KV_LORA, False), (MLA_ROPE, True), (MLA_ROW_W - MLA_CACHE_W, False)])
    tab_nq = _table(cos, sin, [(NSA_HD, True)] * NSA_HEADS)
    tab_nkv = _table(cos, sin, [(2 * NSA_HD, False), (NSA_HD, True), (NSA_HD, False), (NSA_HD, True),
                                (NSA_HD, False)])
    nb_p = s // NSA_BLOCK
    nb_s = past // NSA_BLOCK
    tab_bp = _table(*_rope_tables(jnp.arange(nb_p) * NSA_BLOCK + NSA_BLOCK - 1), [(NSA_HD, True), (NSA_HD, False)])
    tab_bs = _table(*_rope_tables(jnp.arange(nb_s) * NSA_BLOCK + NSA_BLOCK - 1), [(NSA_HD, True), (NSA_HD, False)])

    cache_nsa_v = cache_nsa.reshape(depth, n_phys, page, 4 * NSA_HD)
    x = jnp.concatenate([x_prompt.reshape(n_p, d), x_sample.reshape(n_s, d)], 0)
    ns_tiles = n_s // Q_TILE
    new_p = [[] for _ in range(6)]
    new_s = [[] for _ in range(6)]
    for l in range(depth):
        x = _ffn(x, norm_ffn1[l], *f1, norm_final, l, False)
        z = _inproj(x, norm_mix[l], w_in_p, l)

        zp = z[:, Z_POOL:Z_POOL + BRANCH_W]
        zp_p = zp[:n_p].reshape(b, s, BRANCH_W)
        zp_s = zp[n_p:].reshape(db, t_new, BRANCH_W)
        pre_p = jnp.zeros((b, POOL_HALO, BRANCH_W), F32)
        pre_s = jnp.concatenate([jnp.zeros((db, POOL_HALO - POOL_BUF, BRANCH_W), F32), state_pool[l]], 1)
        o_pool_p = _pool(jnp.concatenate([pre_p, zp_p], 1), pw, pool_scale, l, 0, 1)
        o_pool_s = _pool(jnp.concatenate([pre_s, zp_s], 1), pw, pool_scale, l, past, _pick_tile(db, (32, 16, 8, 1)))
        o_pool = jnp.concatenate([o_pool_p.reshape(n_p, BRANCH_W), o_pool_s.reshape(n_s, BRANCH_W)], 0)
        new_p[3].append(zp_p[:, -POOL_BUF:])
        new_s[3].append(jnp.concatenate([state_pool[l], zp_s], 1)[:, -POOL_BUF:])

        qcat, rows, rows_b = _mla_proj(z[:, Z_MQ:Z_MQ + MLA_Q_LORA], z[:, Z_MKV:Z_MKV + MLA_ROW_W],
                                       mla_q_norm, mla_kv_norm, wuq, wuk, tab_mq + tab_mkv, l)
        o_mla_p = _mla_prompt(qcat[:n_p // Q_TILE], rows_b[:n_p], wuv, l, b, s)
        q_s = _heads_first(qcat[n_p // Q_TILE:], Q_TILE, db, t_new)
        o_mla_s = _mla_sample(page_table, q_s, cache_mla, rows_b[n_p:].reshape(db, t_new, MLA_ROW_W), wuv, l)
        o_mla_all = jnp.concatenate([o_mla_p, o_mla_s.reshape(n_s, BRANCH_W)], 0)
        new_p[0].append(rows[:n_p, :MLA_CACHE_W].reshape(b, s, MLA_CACHE_W))
        new_s[0].append(rows[n_p:, :MLA_CACHE_W].reshape(db, t_new, MLA_CACHE_W))

        nq, nkv, ng = _nsa_proj(z[:, Z_NQ:Z_NQ + BRANCH_W], z[:, Z_NKV:Z_NKV + 6 * NSA_HD],
                                z[:, Z_NG:Z_NG + LANE], tab_nq + tab_nkv)
        kbvb_p = _cmp_proj(_bmean_prompt(nkv[:n_p], b, s), phi, *tab_bp, l)
        o_nsa_p = _nsa_prompt(nq[:n_p // Q_TILE], kbvb_p, nkv[:n_p], ng[:n_p], b, s)
        nq_s = _heads_first(nq[n_p // Q_TILE:], Q_TILE, db, t_new)
        nkv_s = nkv[n_p:].reshape(db, t_new, 6 * NSA_HD)
        kbvb_s = _cmp_proj(_bmean_sample(page_table, cache_nsa_v, l), phi, *tab_bs, l)
        o_cmp_s, sel_s = _nsa_cmp_sample(nq_s, kbvb_s, t_new)
        o_sel_s = _nsa_sel_sample(page_table, nq_s, sel_s, cache_nsa_v, nkv_s[:, :, 2 * NSA_HD:4 * NSA_HD], l, t_new)
        win_buf = state_nsa_win[l].reshape(db, win_len, 2 * NSA_HD)
        o_nsa_s = _nsa_win_sample(nq_s, win_buf, nkv_s[:, :, 4 * NSA_HD:], o_cmp_s, o_sel_s,
                                  ng[n_p:].reshape(db, t_new, LANE), t_new)
        o_nsa_all = jnp.concatenate([o_nsa_p, o_nsa_s.reshape(n_s, BRANCH_W)], 0)
        nkv_p = nkv[:n_p].reshape(b, s, 6, NSA_HD)
        new_p[1].append(nkv_p[:, :, :4])
        new_s[1].append(nkv_s[:, :, :4 * NSA_HD].reshape(db, t_new, 4, NSA_HD))
        new_p[2].append(nkv_p[:, s - win_len:, 4:])
        new_s[2].append(jnp.concatenate([state_nsa_win[l], nkv_s[:, :, 4 * NSA_HD:].reshape(db, t_new, 2, NSA_HD)],
                                        1)[:, -win_len:])

        zr = z[:, Z_RW:Z_RW + RWKV_IN]
        zr_p = zr[:n_p].reshape(b, s, RWKV_IN)
        zr_s = zr[n_p:].reshape(db, t_new, RWKV_IN)
        zprev = jnp.concatenate([
            jnp.concatenate([jnp.zeros((b, 1, RWKV_IN), F32), zr_p[:, :-1]], 1).reshape(n_p, RWKV_IN),
            jnp.concatenate([state_rwkv_shift[l][:, None], zr_s[:, :-1]], 1).reshape(n_s, RWKV_IN)], 0)
        pre = _rwkv_pre(zr, zprev, rwkv_mu, rwkv_w0, rwkv_a0, wa, g2, rwkv_k_k, rwkv_k_a, ones_bd, l)
        r_, k_, v_, lw_, kk_, b_, g_ = pre
        seq_p = lambda a: a[:n_p].reshape(b, s, BRANCH_W)
        seq_s = lambda a: (a[n_p:].reshape(db // samp_grp, samp_grp, t_new, BRANCH_W).transpose(0, 2, 1, 3)
                           .reshape(db // samp_grp, t_new, samp_grp * BRANCH_W))
        scan_in = (r_, k_, v_, lw_, kk_, b_)
        y_p, sf_p = _rwkv_scan(*[seq_p(a) for a in scan_in],
                               jnp.zeros((b, RWKV_HS, BRANCH_W), F32), SCAN_ROWS // RWKV_HEADS)
        s0_s = (state_rwkv[l].reshape(db // samp_grp, samp_grp, RWKV_HEADS, RWKV_HS, RWKV_HS)
                .transpose(0, 3, 1, 2, 4).reshape(db // samp_grp, RWKV_HS, samp_grp * BRANCH_W))
        y_s, sf_s = _rwkv_scan(*[seq_s(a) for a in scan_in], s0_s, samp_c)
        y_s = (y_s.reshape(db // samp_grp, t_new, samp_grp, BRANCH_W).transpose(0, 2, 1, 3)
               .reshape(n_s, BRANCH_W))
        y_all = jnp.concatenate([y_p.reshape(n_p, BRANCH_W), y_s], 0)
        o_rwkv = _rwkv_post(y_all, r_, k_, v_, g_, r_k, rwkv_ln_w, rwkv_ln_b, ones_bd, l)
        new_p[4].append(sf_p.reshape(b, RWKV_HS, RWKV_HEADS, RWKV_HS).transpose(0, 2, 1, 3))
        new_s[4].append(sf_s.reshape(db // samp_grp, RWKV_HS, samp_grp, RWKV_HEADS, RWKV_HS)
                        .transpose(0, 2, 3, 1, 4).reshape(db, RWKV_HEADS, RWKV_HS, RWKV_HS))
        new_p[5].append(zr_p[:, -1])
        new_s[5].append(zr_s[:, -1])

        branches = jnp.concatenate([o_pool, o_mla_all, o_nsa_all, o_rwkv], -1)
        x = _merge(x, branches, z, wb, wo, l)
        x = _ffn(x, norm_ffn2[l], *f2, norm_final, l, l == depth - 1)

    y_prompt = x[:n_p].reshape(b, s, d)
    y_sample = x[n_p:].reshape(db, t_new, d)
    outs_p = [jnp.stack(a, 0) for a in new_p]
    outs_s = [jnp.stack(a, 0) for a in new_s]
    res = [y_prompt, y_sample]
    for a, c in zip(outs_p, outs_s):
        res += [a, c]
    return tuple(res)
```

```python
import functools

import jax
import jax.numpy as jnp
from jax import lax
from jax.experimental import pallas as pl
from jax.experimental.pallas import tpu as pltpu

F32 = jnp.float32
BF16 = jnp.bfloat16

D_MODEL = 2048
N_BRANCH = 4
BRANCH_W = D_MODEL // N_BRANCH
ROPE_THETA = 10000.0
RMS_EPS = 1e-6
NEG = -1e30
POOL_WINDOWS = (2, 4, 8, 16)
POOL_GW = BRANCH_W // len(POOL_WINDOWS)
POOL_BUF = max(POOL_WINDOWS) - 1
POOL_HALO = 16
MLA_HEADS = 4
MLA_NOPE = 128
MLA_ROPE = 64
MLA_V = BRANCH_W // MLA_HEADS
MLA_Q_LORA = D_MODEL // 4
MLA_KV_LORA = D_MODEL // 8
MLA_CACHE_W = MLA_KV_LORA + MLA_ROPE
MLA_ROW_W = 384
MLA_SCALE = (MLA_NOPE + MLA_ROPE) ** -0.5
NSA_HEADS = 8
NSA_HD = BRANCH_W // NSA_HEADS
NSA_BLOCK = 64
NSA_TOPN = 16
NSA_WINDOW = 512
NSA_SCALE = NSA_HD ** -0.5
RWKV_HS = 64
RWKV_HEADS = BRANCH_W // RWKV_HS
RWKV_W_LORA = 64
RWKV_A_LORA = 64
RWKV_G_LORA = 128
RWKV_GN_EPS = 64e-5
RWKV_IN = 3 * BRANCH_W + RWKV_W_LORA + RWKV_A_LORA + RWKV_G_LORA
SCAN_ROWS = 256
Q_TILE = 128
LANE = 128

Z_GATE = 0
Z_POOL = N_BRANCH * D_MODEL
Z_MQ = Z_POOL + BRANCH_W
Z_MKV = Z_MQ + MLA_Q_LORA
Z_NQ = Z_MKV + MLA_ROW_W
Z_NKV = Z_NQ + BRANCH_W
Z_NG = Z_NKV + 6 * NSA_HD
Z_RW = Z_NG + LANE
Z_END = Z_RW + RWKV_IN
Z_TILE = 512
Z_W = -(-Z_END // Z_TILE) * Z_TILE


def _cparams(sem, vmem_mb=48):
    return pltpu.CompilerParams(dimension_semantics=sem, vmem_limit_bytes=vmem_mb * 1024 * 1024)


def _pick_tile(n, cands):
    for c in cands:
        if n % c == 0:
            return c
    raise ValueError(f"no tile in {cands} divides {n}")


def _vec3(a):
    return a.reshape(a.shape[0], 1, a.shape[1])


def _vec_spec(w, l, ngrid):
    if ngrid == 1:
        return pl.BlockSpec((None, 1, w), lambda i: (l, 0, 0))
    return pl.BlockSpec((None, 1, w), lambda i, j: (l, 0, 0))


def _dot(a, b):
    return jnp.dot(a.astype(BF16), b.astype(BF16), preferred_element_type=F32)


def _dot_nt(a, b):
    return lax.dot_general(a.astype(BF16), b.astype(BF16), (((1,), (1,)), ((), ())),
                           preferred_element_type=F32)


def _dot_split(a, b01, terms=2):
    acc = None
    rem = a
    for _ in range(terms):
        piece = rem.astype(BF16)
        d = jnp.dot(piece, b01, preferred_element_type=F32)
        acc = d if acc is None else acc + d
        rem = rem - piece.astype(F32)
    return acc


def _rms(x, g):
    return x * lax.rsqrt(jnp.mean(x * x, -1, keepdims=True) + RMS_EPS) * g


def _rope_apply(x, cos, sin):
    w = x.shape[-1]
    lane = lax.broadcasted_iota(jnp.int32, x.shape, x.ndim - 1)
    fwd = pltpu.roll(x, w - 32, x.ndim - 1)
    bwd = pltpu.roll(x, 32, x.ndim - 1)
    return x * cos + jnp.where((lane & 63) < 32, fwd, bwd) * sin


def _ffn_body(x_ref, g_ref, wg_ref, wu_ref, wd_ref, gf_ref, o_ref, h_ref, acc_ref, *, nj, final_norm):
    j = pl.program_id(1)

    @pl.when(j == 0)
    def _():
        h_ref[...] = _rms(x_ref[...], g_ref[...]).astype(BF16)

    h = h_ref[...]
    gt = jnp.dot(h, wg_ref[...], preferred_element_type=F32)
    up = jnp.dot(h, wu_ref[...], preferred_element_type=F32)
    d = _dot(gt * jax.nn.sigmoid(gt) * up, wd_ref[...])

    @pl.when(j == 0)
    def _():
        acc_ref[...] = d

    @pl.when(j > 0)
    def _():
        acc_ref[...] += d

    @pl.when(j == nj - 1)
    def _():
        y = x_ref[...] + 0.5 * acc_ref[...]
        if final_norm:
            y = _rms(y, gf_ref[...])
        o_ref[...] = y


def _ffn(x, g, wg, wu, wd, gf, l, final_norm):
    n, d = x.shape
    f = wg.shape[2]
    tm = _pick_tile(n, (512, 256, 128))
    tf = _pick_tile(f, (512, 256, 128))
    nj = f // tf
    return pl.pallas_call(
        functools.partial(_ffn_body, nj=nj, final_norm=final_norm),
        grid=(n // tm, nj),
        in_specs=[
            pl.BlockSpec((tm, d), lambda i, j: (i, 0)),
            pl.BlockSpec((1, d), lambda i, j: (0, 0)),
            pl.BlockSpec((None, d, tf), lambda i, j: (l, 0, j)),
            pl.BlockSpec((None, d, tf), lambda i, j: (l, 0, j)),
            pl.BlockSpec((None, tf, d), lambda i, j: (l, j, 0)),
            pl.BlockSpec((1, d), lambda i, j: (0, 0)),
        ],
        out_specs=pl.BlockSpec((tm, d), lambda i, j: (i, 0)),
        out_shape=jax.ShapeDtypeStruct((n, d), F32),
        scratch_shapes=[pltpu.VMEM((tm, d), BF16), pltpu.VMEM((tm, d), F32)],
        compiler_params=_cparams(("parallel", "arbitrary")),
        name="ffn",
    )(x, g.reshape(1, d), wg, wu, wd, gf.reshape(1, d))


def _inproj_body(x_ref, g_ref, w_ref, o_ref, h_ref):
    @pl.when(pl.program_id(1) == 0)
    def _():
        h_ref[...] = _rms(x_ref[...], g_ref[...]).astype(BF16)

    o_ref[...] = jnp.dot(h_ref[...], w_ref[...], preferred_element_type=F32)


def _inproj(x, g, w, l):
    n, d = x.shape
    zw = w.shape[2]
    tm = _pick_tile(n, (512, 256, 128))
    return pl.pallas_call(
        _inproj_body,
        grid=(n // tm, zw // Z_TILE),
        in_specs=[
            pl.BlockSpec((tm, d), lambda i, j: (i, 0)),
            pl.BlockSpec((1, d), lambda i, j: (0, 0)),
            pl.BlockSpec((None, d, Z_TILE), lambda i, j: (l, 0, j)),
        ],
        out_specs=pl.BlockSpec((tm, Z_TILE), lambda i, j: (i, j)),
        out_shape=jax.ShapeDtypeStruct((n, zw), F32),
        scratch_shapes=[pltpu.VMEM((tm, d), BF16)],
        compiler_params=_cparams(("parallel", "arbitrary")),
        name="inproj",
    )(x, g.reshape(1, d), w)


def _merge_body(x_ref, br_ref, zg_ref, wb_ref, wo_ref, o_ref, acc_ref):
    n = pl.program_id(1)
    t = jax.nn.sigmoid(zg_ref[...]) * _dot(br_ref[...], wb_ref[...])

    @pl.when(n == 0)
    def _():
        acc_ref[...] = t

    @pl.when(n > 0)
    def _():
        acc_ref[...] += t

    @pl.when(n == N_BRANCH - 1)
    def _():
        o_ref[...] = x_ref[...] + _dot(acc_ref[...], wo_ref[...])


def _merge(x, branches, z, wb, wo, l):
    n, d = x.shape
    tm = _pick_tile(n, (256, 128))
    return pl.pallas_call(
        _merge_body,
        grid=(n // tm, N_BRANCH),
        in_specs=[
            pl.BlockSpec((tm, d), lambda i, k: (i, 0)),
            pl.BlockSpec((tm, BRANCH_W), lambda i, k: (i, k)),
            pl.BlockSpec((tm, d), lambda i, k: (i, k)),
            pl.BlockSpec((None, None, BRANCH_W, d), lambda i, k: (l, k, 0, 0)),
            pl.BlockSpec((None, d, d), lambda i, k: (l, 0, 0)),
        ],
        out_specs=pl.BlockSpec((tm, d), lambda i, k: (i, 0)),
        out_shape=jax.ShapeDtypeStruct((n, d), F32),
        scratch_shapes=[pltpu.VMEM((tm, d), F32)],
        compiler_params=_cparams(("parallel", "arbitrary")),
        name="merge",
    )(x, branches, z, wb, wo)


def _pool_body(u_ref, w_ref, sc_ref, o_ref, *, t_len, t_chunk, pos0):
    g = pl.program_id(1)
    bt = u_ref.shape[0]
    for gi, win in enumerate(POOL_WINDOWS):
        @pl.when(g == gi)
        def _(win=win):
            for c0 in range(0, t_len, t_chunk):
                ue = u_ref[:, c0:c0 + t_chunk + POOL_HALO, :]
                acc = ue
                span = 1
                while span < win:
                    acc = acc[:, span:, :] + acc[:, :-span, :]
                    span *= 2
                wsum = acc[:, POOL_HALO + 1 - win:POOL_HALO + 1 - win + t_chunk, :]
                pos = pos0 + c0 + lax.broadcasted_iota(jnp.int32, (1, t_chunk, 1), 1)
                cnt = jnp.minimum(win, pos + 1).astype(F32)
                diff = wsum / cnt - ue[:, POOL_HALO:, :]
                y = _dot(diff.reshape(bt * t_chunk, LANE), w_ref[...]) * sc_ref[...]
                o_ref[:, c0:c0 + t_chunk, :] = y.reshape(bt, t_chunk, LANE)


def _pool(u_ext, w, scale, l, pos0, bt):
    b, te, c = u_ext.shape
    t_len = te - POOL_HALO
    t_chunk = _pick_tile(t_len, (512, 256, 128, 8))
    return pl.pallas_call(
        functools.partial(_pool_body, t_len=t_len, t_chunk=t_chunk, pos0=pos0),
        grid=(b // bt, len(POOL_WINDOWS)),
        in_specs=[
            pl.BlockSpec((bt, te, LANE), lambda i, g: (i, 0, g)),
            pl.BlockSpec((None, None, POOL_GW, POOL_GW), lambda i, g: (l, g, 0, 0)),
            pl.BlockSpec((None, 1, LANE), lambda i, g: (l, 0, g)),
        ],
        out_specs=pl.BlockSpec((bt, t_len, LANE), lambda i, g: (i, 0, g)),
        out_shape=jax.ShapeDtypeStruct((b, t_len, c), F32),
        compiler_params=_cparams(("parallel", "arbitrary")),
        name="pool",
    )(u_ext, w, _vec3(scale))


def _mla_proj_body(zq_ref, zkv_ref, qn_ref, kvn_ref, wuq_ref, wuk_ref, cq_ref, sq_ref, ckv_ref, skv_ref,
                   qcat_ref, qcat_t_ref, rows_ref, rowsb_ref, rows_t_ref):
    tm = zq_ref.shape[0]
    zk = zkv_ref[...]
    ckv = _rms(zk[:, :MLA_KV_LORA], kvn_ref[...])
    roped = _rope_apply(zk, ckv_ref[...], skv_ref[...])
    rows = jnp.concatenate([ckv, roped[:, MLA_KV_LORA:]], -1)
    rows_ref[...] = rows
    rowsb_ref[...] = rows.astype(BF16)
    rows_t_ref[0] = rows.T.astype(BF16)
    q = _dot(_rms(zq_ref[...], qn_ref[...]), wuq_ref[...])
    q = _rope_apply(q, cq_ref[...], sq_ref[...])
    pad = jnp.zeros((tm, MLA_ROW_W - MLA_CACHE_W), F32)
    for h in range(MLA_HEADS):
        q_lat = _dot(q[:, h * MLA_NOPE:(h + 1) * MLA_NOPE], wuk_ref[h])
        q_pe = q[:, MLA_HEADS * MLA_NOPE + h * MLA_ROPE:MLA_HEADS * MLA_NOPE + (h + 1) * MLA_ROPE]
        qcat = jnp.concatenate([q_lat, q_pe, pad], -1) * MLA_SCALE
        qcat_ref[0, h] = qcat.astype(BF16)
        qcat_t_ref[0, :, h * tm:(h + 1) * tm] = qcat.T.astype(BF16)


def _mla_proj(zq, zkv, qn, kvn, wuq, wuk, tabs, l):
    n = zq.shape[0]
    tm = Q_TILE
    qw = wuq.shape[2]
    row = lambda w: pl.BlockSpec((tm, w), lambda i: (i, 0))
    return pl.pallas_call(
        _mla_proj_body,
        grid=(n // tm,),
        in_specs=[
            row(MLA_Q_LORA), row(MLA_ROW_W),
            _vec_spec(MLA_Q_LORA, l, 1), _vec_spec(MLA_KV_LORA, l, 1),
            pl.BlockSpec((None, MLA_Q_LORA, qw), lambda i: (l, 0, 0)),
            pl.BlockSpec((None, MLA_HEADS, MLA_NOPE, MLA_KV_LORA), lambda i: (l, 0, 0, 0)),
            row(qw), row(qw), row(MLA_ROW_W), row(MLA_ROW_W),
        ],
        out_specs=[
            pl.BlockSpec((1, MLA_HEADS, tm, MLA_ROW_W), lambda i: (i, 0, 0, 0)),
            pl.BlockSpec((1, MLA_ROW_W, MLA_HEADS * tm), lambda i: (i, 0, 0)),
            row(MLA_ROW_W), row(MLA_ROW_W),
            pl.BlockSpec((1, MLA_ROW_W, tm), lambda i: (i, 0, 0)),
        ],
        out_shape=[
            jax.ShapeDtypeStruct((n // tm, MLA_HEADS, tm, MLA_ROW_W), BF16),
            jax.ShapeDtypeStruct((n // tm, MLA_ROW_W, MLA_HEADS * tm), BF16),
            jax.ShapeDtypeStruct((n, MLA_ROW_W), F32),
            jax.ShapeDtypeStruct((n, MLA_ROW_W), BF16),
            jax.ShapeDtypeStruct((n // tm, MLA_ROW_W, tm), BF16),
        ],
        compiler_params=_cparams(("parallel",)),
        name="mla_proj",
    )(zq, zkv, _vec3(qn), _vec3(kvn), wuq, wuk, *tabs)


def _masked(x, mask, heads, fill):
    if mask is None:
        return x
    tq, tk = mask.shape
    return jnp.where(mask[None], x.reshape(heads, tq, tk), fill).reshape(heads * tq, tk)


def _softmax_step(s, mask, heads, m_ref, l_ref, acc_ref, v, vt=False):
    s = _masked(s, mask, heads, NEG)
    m_old = m_ref[...]
    m_new = jnp.maximum(m_old, jnp.max(s, -1, keepdims=True))
    alpha = jnp.exp(m_old - m_new)
    p = _masked(jnp.exp(s - m_new), mask, heads, 0.0)
    l_ref[...] = alpha * l_ref[...] + jnp.sum(p, -1, keepdims=True)
    acc_ref[...] = alpha * acc_ref[...] + (_dot_nt(p, v) if vt else _dot(p, v))
    m_ref[...] = m_new


def _softmax_once(s, mask, heads, v, vt=False):
    s = _masked(s, mask, heads, NEG)
    e = _masked(jnp.exp(s - jnp.max(s, -1, keepdims=True)), mask, heads, 0.0)
    den = jnp.sum(e, -1, keepdims=True)
    p = e / jnp.where(den > 0, den, 1.0)
    return (_dot_nt(p, v) if vt else _dot(p, v)), p


def _lanes(a, heads):
    return jnp.concatenate([a] * heads, 1)


def _softmax_step_t(st, keep, m_ref, l_ref, acc_ref, vt):
    if keep is not None:
        keep = keep > 0.5
        st = jnp.where(keep, st, NEG)
    m_old = m_ref[...]
    m_new = jnp.maximum(m_old, jnp.max(st, 0, keepdims=True))
    alpha = jnp.exp(m_old - m_new)
    p = jnp.exp(st - m_new)
    if keep is not None:
        p = jnp.where(keep, p, 0.0)
    l_ref[...] = alpha * l_ref[...] + jnp.sum(p, 0, keepdims=True)
    acc_ref[...] = alpha * acc_ref[...] + _dot(vt, p)
    m_ref[...] = m_new


def _softmax_once_t(st, keep, vt):
    keep = keep > 0.5
    st = jnp.where(keep, st, NEG)
    e = jnp.where(keep, jnp.exp(st - jnp.max(st, 0, keepdims=True)), 0.0)
    den = jnp.sum(e, 0, keepdims=True)
    p = e / jnp.where(den > 0, den, 1.0)
    return _dot(vt, p), p


def _softmax_init(m_ref, l_ref, acc_ref):
    m_ref[...] = jnp.full(m_ref.shape, NEG, F32)
    l_ref[...] = jnp.zeros(l_ref.shape, F32)
    acc_ref[...] = jnp.zeros(acc_ref.shape, F32)


def _softmax_out(l_ref, acc_ref):
    l = l_ref[...]
    return acc_ref[...] / jnp.where(l > 0, l, 1.0)


def _mla_prompt_body(qt_ref, k_ref, kt_ref, wuv_ref, o_ref, m_ref, l_ref, acc_ref):
    i = pl.program_id(1)
    tq = kt_ref.shape[2]
    qt = qt_ref[0]
    _softmax_init(m_ref, l_ref, acc_ref)

    def tile(j):
        return k_ref[0, pl.ds(pl.multiple_of(j * tq, tq), tq), :], kt_ref[j, :MLA_KV_LORA, :]

    def step(j, carry):
        k, vt = tile(j)
        _softmax_step_t(_dot(k, qt), None, m_ref, l_ref, acc_ref, vt)
        return carry

    lax.fori_loop(0, i, step, 0)
    k, vt = tile(i)
    causal = (lax.broadcasted_iota(jnp.int32, (tq, tq), 0) <= lax.broadcasted_iota(jnp.int32, (tq, tq), 1))
    _softmax_step_t(_dot(k, qt), _lanes(jnp.where(causal, 1.0, 0.0), MLA_HEADS), m_ref, l_ref, acc_ref, vt)
    o_lat_t = acc_ref[...] / l_ref[...]
    for h in range(MLA_HEADS):
        o_ref[:, h * MLA_V:(h + 1) * MLA_V] = _dot(o_lat_t[:, h * tq:(h + 1) * tq].T, wuv_ref[h])


def _mla_prompt(qcat_t, rows_b, rows_t, wuv, l, b, s):
    tq = Q_TILE
    nq = s // tq
    rows = MLA_HEADS * tq
    return pl.pallas_call(
        _mla_prompt_body,
        grid=(b, nq),
        in_specs=[
            pl.BlockSpec((1, MLA_ROW_W, rows), lambda bi, i: (bi * nq + i, 0, 0)),
            pl.BlockSpec((1, s, MLA_ROW_W), lambda bi, i: (bi, 0, 0)),
            pl.BlockSpec((nq, MLA_ROW_W, tq), lambda bi, i: (bi, 0, 0)),
            pl.BlockSpec((None, MLA_HEADS, MLA_KV_LORA, MLA_V), lambda bi, i: (l, 0, 0, 0)),
        ],
        out_specs=pl.BlockSpec((tq, BRANCH_W), lambda bi, i: (bi * nq + i, 0)),
        out_shape=jax.ShapeDtypeStruct((b * s, BRANCH_W), F32),
        scratch_shapes=[pltpu.VMEM((1, rows), F32), pltpu.VMEM((1, rows), F32),
                        pltpu.VMEM((MLA_KV_LORA, rows), F32)],
        compiler_params=_cparams(("parallel", "arbitrary")),
        name="mla_prompt",
    )(qcat_t, rows_b.reshape(b, s, MLA_ROW_W), rows_t, wuv)


def _mla_sample_body(pt_ref, q_ref, *refs, pc, t_new):
    page_refs = refs[:pc]
    new_ref, wuv_ref, o_ref, m_ref, l_ref, acc_ref = refs[pc:]
    c = pl.program_id(1)

    @pl.when(c == 0)
    def _():
        _softmax_init(m_ref, l_ref, acc_ref)

    q = q_ref[0][:, :MLA_CACHE_W]
    keys_t = jnp.concatenate([r[...].astype(BF16) for r in page_refs], 1)
    _softmax_step(_dot(q, keys_t), None, MLA_HEADS, m_ref, l_ref, acc_ref, keys_t[:MLA_KV_LORA], vt=True)

    @pl.when(c == pl.num_programs(1) - 1)
    def _():
        k_new = new_ref[0][:, :MLA_CACHE_W]
        tq = lax.broadcasted_iota(jnp.int32, (t_new, t_new), 0)
        tk = lax.broadcasted_iota(jnp.int32, (t_new, t_new), 1)
        _softmax_step(_dot_nt(q, k_new), tk <= tq, MLA_HEADS, m_ref, l_ref, acc_ref, k_new[:, :MLA_KV_LORA])
        o_lat = _softmax_out(l_ref, acc_ref)
        for h in range(MLA_HEADS):
            o_ref[0, :, h * MLA_V:(h + 1) * MLA_V] = _dot(o_lat[h * t_new:(h + 1) * t_new], wuv_ref[h])


def _page_chunk(n_pages):
    return _pick_tile(n_pages, (32, 16, 8, 4, 2, 1))


def _mla_sample(page_table, q, cache_t, rows_new, wuv, l):
    db, rows, _ = q.shape
    t_new = rows // MLA_HEADS
    n_pages = page_table.shape[1]
    page = cache_t.shape[3]
    pc = _page_chunk(n_pages)

    def page_spec(k):
        return pl.BlockSpec((None, None, MLA_CACHE_W, page),
                            lambda bi, c, pt: (l, pt[bi, c * pc + k], 0, 0))

    grid_spec = pltpu.PrefetchScalarGridSpec(
        num_scalar_prefetch=1,
        grid=(db, n_pages // pc),
        in_specs=[pl.BlockSpec((1, rows, MLA_ROW_W), lambda bi, c, pt: (bi, 0, 0))]
        + [page_spec(k) for k in range(pc)]
        + [pl.BlockSpec((1, t_new, MLA_ROW_W), lambda bi, c, pt: (bi, 0, 0)),
           pl.BlockSpec((None, MLA_HEADS, MLA_KV_LORA, MLA_V), lambda bi, c, pt: (l, 0, 0, 0))],
        out_specs=pl.BlockSpec((1, t_new, BRANCH_W), lambda bi, c, pt: (bi, 0, 0)),
        scratch_shapes=[pltpu.VMEM((rows, 1), F32), pltpu.VMEM((rows, 1), F32),
                        pltpu.VMEM((rows, MLA_KV_LORA), F32)],
    )
    return pl.pallas_call(
        functools.partial(_mla_sample_body, pc=pc, t_new=t_new),
        grid_spec=grid_spec,
        out_shape=jax.ShapeDtypeStruct((db, t_new, BRANCH_W), F32),
        compiler_params=_cparams(("parallel", "arbitrary")),
        name="mla_sample",
    )(page_table, q, *([cache_t] * pc), rows_new, wuv)


def _nsa_proj_body(zq_ref, zkv_ref, zg_ref, cq_ref, sq_ref, ckv_ref, skv_ref, q_ref, qt_ref, kv_ref, g_ref):
    tm = zq_ref.shape[0]
    q = _rope_apply(zq_ref[...], cq_ref[...], sq_ref[...]) * NSA_SCALE
    q_t = q.T
    for h in range(NSA_HEADS):
        q_ref[0, h] = q[:, h * NSA_HD:(h + 1) * NSA_HD].astype(BF16)
        qt_ref[0, :, h * tm:(h + 1) * tm] = q_t[h * NSA_HD:(h + 1) * NSA_HD].astype(BF16)
    kv_ref[...] = _rope_apply(zkv_ref[...], ckv_ref[...], skv_ref[...])
    g_ref[...] = jax.nn.sigmoid(zg_ref[...])


def _nsa_proj(zq, zkv, zg, tabs):
    n = zq.shape[0]
    tm = Q_TILE
    kvw = zkv.shape[1]
    row = lambda w: pl.BlockSpec((tm, w), lambda i: (i, 0))
    return pl.pallas_call(
        _nsa_proj_body,
        grid=(n // tm,),
        in_specs=[row(BRANCH_W), row(kvw), row(LANE), row(BRANCH_W), row(BRANCH_W), row(kvw), row(kvw)],
        out_specs=[pl.BlockSpec((1, NSA_HEADS, tm, NSA_HD), lambda i: (i, 0, 0, 0)),
                   pl.BlockSpec((1, NSA_HD, NSA_HEADS * tm), lambda i: (i, 0, 0)), row(kvw), row(LANE)],
        out_shape=[
            jax.ShapeDtypeStruct((n // tm, NSA_HEADS, tm, NSA_HD), BF16),
            jax.ShapeDtypeStruct((n // tm, NSA_HD, NSA_HEADS * tm), BF16),
            jax.ShapeDtypeStruct((n, kvw), F32),
            jax.ShapeDtypeStruct((n, LANE), F32),
        ],
        compiler_params=_cparams(("parallel",)),
        name="nsa_proj",
    )(zq, zkv, zg, *tabs)


def _bmean_prompt_body(kv_ref, o_ref):
    nb = o_ref.shape[1]
    x = kv_ref[0][:nb * NSA_BLOCK]
    o_ref[0] = jnp.sum(x.reshape(nb, NSA_BLOCK, LANE), 1) * (1.0 / NSA_BLOCK)


def _bmean_prompt(kv, b, s):
    nb = s // NSA_BLOCK
    return pl.pallas_call(
        _bmean_prompt_body,
        grid=(b,),
        in_specs=[pl.BlockSpec((1, s, LANE), lambda bi: (bi, 0, 0))],
        out_specs=pl.BlockSpec((1, nb, LANE), lambda bi: (bi, 0, 0)),
        out_shape=jax.ShapeDtypeStruct((b, nb, LANE), F32),
        compiler_params=_cparams(("parallel",)),
        name="bmean_prompt",
    )(kv.reshape(b, s, kv.shape[-1]))


def _bmean_sample_body(pt_ref, *refs, pc):
    a_ref, o_ref = refs[pc], refs[pc + 1]
    x = jnp.concatenate([r[...].astype(BF16) for r in refs[:pc]], 1)
    o_ref[0, 0] = jnp.dot(x, a_ref[...], preferred_element_type=F32) * (1.0 / NSA_BLOCK)


def _bmean_sample(page_table, cache_t, l):
    db, n_pages = page_table.shape
    page = cache_t.shape[3]
    pc = _page_chunk(n_pages)
    per = page // NSA_BLOCK
    nch = n_pages // pc
    ind = (jnp.arange(pc * page)[:, None] // NSA_BLOCK == jnp.arange(pc * per)[None, :]).astype(BF16)

    def page_spec(k):
        return pl.BlockSpec((None, None, 2 * NSA_HD, page), lambda bi, c, pt: (l, pt[bi, c * pc + k], 0, 0))

    grid_spec = pltpu.PrefetchScalarGridSpec(
        num_scalar_prefetch=1,
        grid=(db, nch),
        in_specs=[page_spec(k) for k in range(pc)]
        + [pl.BlockSpec((pc * page, pc * per), lambda bi, c, pt: (0, 0))],
        out_specs=pl.BlockSpec((1, 1, 2 * NSA_HD, pc * per), lambda bi, c, pt: (bi, c, 0, 0)),
    )
    out = pl.pallas_call(
        functools.partial(_bmean_sample_body, pc=pc),
        grid_spec=grid_spec,
        out_shape=jax.ShapeDtypeStruct((db, nch, 2 * NSA_HD, pc * per), F32),
        compiler_params=_cparams(("parallel", "arbitrary")),
        name="bmean_sample",
    )(page_table, *([cache_t] * pc), ind)
    return out.transpose(0, 2, 1, 3).reshape(db, 2 * NSA_HD, n_pages * per)


def _cmp_proj_t_body(x_ref, phi_ref, c_ref, s_ref, o_ref):
    half = NSA_HD // 2
    for bi in range(x_ref.shape[0]):
        y = _dot(phi_ref[...], x_ref[bi])
        yk = y[:NSA_HD]
        swapped = jnp.concatenate([yk[half:], yk[:half]], 0)
        o_ref[bi] = jnp.concatenate([yk * c_ref[...] + swapped * s_ref[...], y[NSA_HD:]], 0)


def _cmp_proj_t(means_t, phi_t, cos_t, sin_t, l):
    b, _, nb = means_t.shape
    bt = _pick_tile(b, (8, 4, 2, 1))
    return pl.pallas_call(
        _cmp_proj_t_body,
        grid=(b // bt,),
        in_specs=[
            pl.BlockSpec((bt, LANE, nb), lambda i: (i, 0, 0)),
            pl.BlockSpec((None, LANE, LANE), lambda i: (l, 0, 0)),
            pl.BlockSpec((NSA_HD, nb), lambda i: (0, 0)),
            pl.BlockSpec((NSA_HD, nb), lambda i: (0, 0)),
        ],
        out_specs=pl.BlockSpec((bt, LANE, nb), lambda i: (i, 0, 0)),
        out_shape=jax.ShapeDtypeStruct((b, LANE, nb), F32),
        compiler_params=_cparams(("parallel",)),
        name="cmp_proj_t",
    )(means_t, phi_t, cos_t, sin_t)


def _cmp_proj_body(x_ref, phi_ref, c_ref, s_ref, o_ref):
    bt, nb, _ = x_ref.shape
    y = _dot(x_ref[...].reshape(bt * nb, LANE), phi_ref[...]).reshape(bt, nb, LANE)
    o_ref[...] = _rope_apply(y, c_ref[...][None], s_ref[...][None])


def _cmp_proj(means, phi, cos, sin, l):
    b, nb, _ = means.shape
    bt = _pick_tile(b, (8, 4, 2, 1))
    return pl.pallas_call(
        _cmp_proj_body,
        grid=(b // bt,),
        in_specs=[
            pl.BlockSpec((bt, nb, LANE), lambda i: (i, 0, 0)),
            pl.BlockSpec((None, LANE, LANE), lambda i: (l, 0, 0)),
            pl.BlockSpec((nb, LANE), lambda i: (0, 0)),
            pl.BlockSpec((nb, LANE), lambda i: (0, 0)),
        ],
        out_specs=pl.BlockSpec((bt, nb, LANE), lambda i: (i, 0, 0)),
        out_shape=jax.ShapeDtypeStruct((b, nb, LANE), F32),
        compiler_params=_cparams(("parallel",)),
        name="cmp_proj",
    )(means, phi, cos, sin)


def _topk_select(imp, cand, k, axis):
    nb = imp.shape[axis]
    idx = lax.broadcasted_iota(jnp.int32, imp.shape, axis).astype(F32)
    v = imp if cand is None else jnp.where(cand, imp, -1.0)
    sel = jnp.zeros(imp.shape, F32)
    for _ in range(k):
        m = jnp.max(v, axis, keepdims=True)
        first = jnp.min(jnp.where(v == m, idx, float(nb)), axis, keepdims=True)
        hit = idx == first
        sel = jnp.where(hit & (m >= 0.0), 1.0, sel)
        v = jnp.where(hit, -2.0, v)
    return sel


def _nsa_prompt_body(qt_ref, kbvb_ref, kv_ref, g_ref, o_ref, m_ref, l_ref, acc_ref):
    i = pl.program_id(1)
    heads = NSA_HEADS
    tq = qt_ref.shape[2] // heads
    nb = kbvb_ref.shape[1]
    qt = qt_ref[0]
    kbvb = kbvb_ref[0]

    blk = lax.broadcasted_iota(jnp.int32, (nb, tq), 0)
    qpos_b = i * tq + lax.broadcasted_iota(jnp.int32, (nb, tq), 1)
    vis = jnp.where(blk * NSA_BLOCK + (NSA_BLOCK - 1) <= qpos_b, 1.0, 0.0)
    o_cmp, p = _softmax_once_t(_dot(kbvb[:, :NSA_HD], qt), _lanes(vis, heads), kbvb[:, NSA_HD:].T)
    imp = p[:, :tq]
    for h in range(1, heads):
        imp = imp + p[:, h * tq:(h + 1) * tq]
    cur = qpos_b // NSA_BLOCK
    sel = jnp.where(blk == cur, 1.0, _topk_select(imp, blk < cur, min(NSA_TOPN - 1, nb), 0))

    koff = lax.broadcasted_iota(jnp.int32, (tq, tq), 0)
    qpos = i * tq + lax.broadcasted_iota(jnp.int32, (tq, tq), 1)
    per = tq // NSA_BLOCK
    e_key = lax.broadcasted_iota(jnp.int32, (tq, nb), 0) // NSA_BLOCK
    e_blk = lax.broadcasted_iota(jnp.int32, (tq, nb), 1)

    def sel_step(j, carry):
        kv = kv_ref[0, pl.ds(pl.multiple_of(j * tq, tq), tq), 2 * NSA_HD:4 * NSA_HD]
        chosen = _dot(jnp.where(e_blk == j * per + e_key, 1.0, 0.0), sel)
        keep = jnp.where((chosen > 0.5) & (j * tq + koff <= qpos), 1.0, 0.0)
        _softmax_step_t(_dot(kv[:, :NSA_HD], qt), _lanes(keep, heads), m_ref, l_ref, acc_ref, kv[:, NSA_HD:].T)
        return carry

    _softmax_init(m_ref, l_ref, acc_ref)
    lax.fori_loop(0, i + 1, sel_step, 0)
    o_sel = acc_ref[...] / l_ref[...]

    def win_step(j, carry):
        kv = kv_ref[0, pl.ds(pl.multiple_of(j * tq, tq), tq), 4 * NSA_HD:6 * NSA_HD]
        rel = qpos - (j * tq + koff)
        keep = jnp.where((rel >= 0) & (rel < NSA_WINDOW), 1.0, 0.0)
        _softmax_step_t(_dot(kv[:, :NSA_HD], qt), _lanes(keep, heads), m_ref, l_ref, acc_ref, kv[:, NSA_HD:].T)
        return carry

    _softmax_init(m_ref, l_ref, acc_ref)
    lax.fori_loop(jnp.maximum(i - NSA_WINDOW // tq, 0), i + 1, win_step, 0)
    o_win = acc_ref[...] / l_ref[...]

    g_t = g_ref[...].T
    outs = []
    for h in range(heads):
        cols = slice(h * tq, (h + 1) * tq)
        outs.append(g_t[h:h + 1] * o_cmp[:, cols] + g_t[heads + h:heads + h + 1] * o_sel[:, cols]
                    + g_t[2 * heads + h:2 * heads + h + 1] * o_win[:, cols])
    o_ref[...] = jnp.concatenate(outs, 0).T


def _nsa_prompt(q_t, kbvb, kv, gates, b, s):
    tq = Q_TILE
    nq = s // tq
    nb = kbvb.shape[1]
    kvw = kv.shape[-1]
    rows = NSA_HEADS * tq
    return pl.pallas_call(
        _nsa_prompt_body,
        grid=(b, nq),
        in_specs=[
            pl.BlockSpec((1, NSA_HD, rows), lambda bi, i: (bi * nq + i, 0, 0)),
            pl.BlockSpec((1, nb, LANE), lambda bi, i: (bi, 0, 0)),
            pl.BlockSpec((1, s, kvw), lambda bi, i: (bi, 0, 0)),
            pl.BlockSpec((tq, LANE), lambda bi, i: (bi * nq + i, 0)),
        ],
        out_specs=pl.BlockSpec((tq, BRANCH_W), lambda bi, i: (bi * nq + i, 0)),
        out_shape=jax.ShapeDtypeStruct((b * s, BRANCH_W), F32),
        scratch_shapes=[pltpu.VMEM((1, rows), F32), pltpu.VMEM((1, rows), F32), pltpu.VMEM((NSA_HD, rows), F32)],
        compiler_params=_cparams(("parallel", "arbitrary")),
        name="nsa_prompt",
    )(q_t, kbvb, kv.reshape(b, s, kvw), gates)


def _nsa_cmp_sample_body(q_ref, kt_ref, o_ref, sel_ref, *, t_new):
    bt, _, nb = kt_ref.shape
    imps = []
    for bi in range(bt):
        o, p = _softmax_once(_dot(q_ref[bi], kt_ref[bi, :NSA_HD, :]), None, NSA_HEADS, kt_ref[bi, NSA_HD:, :],
                             vt=True)
        o_ref[bi] = o
        imps.append(jnp.sum(p.reshape(NSA_HEADS, t_new, nb), 0))
    sel = _topk_select(jnp.concatenate(imps, 0), None, min(NSA_TOPN - 1, nb), 1)
    sel_ref[...] = sel.reshape(bt, t_new, nb)


def _nsa_cmp_sample(q, kbvb_t, t_new):
    db, rows, _ = q.shape
    nb = kbvb_t.shape[2]
    bt = _pick_tile(db, (8, 4, 2, 1))
    return pl.pallas_call(
        functools.partial(_nsa_cmp_sample_body, t_new=t_new),
        grid=(db // bt,),
        in_specs=[pl.BlockSpec((bt, rows, NSA_HD), lambda bi: (bi, 0, 0)),
                  pl.BlockSpec((bt, LANE, nb), lambda bi: (bi, 0, 0))],
        out_specs=[pl.BlockSpec((bt, rows, NSA_HD), lambda bi: (bi, 0, 0)),
                   pl.BlockSpec((bt, t_new, nb), lambda bi: (bi, 0, 0))],
        out_shape=[jax.ShapeDtypeStruct((db, rows, NSA_HD), F32), jax.ShapeDtypeStruct((db, t_new, nb), F32)],
        compiler_params=_cparams(("parallel",)),
        name="nsa_cmp_sample",
    )(q, kbvb_t)


def _nsa_sel_sample_body(pt_ref, q_ref, sel_ref, exp_ref, *refs, pc, t_new):
    page_refs = refs[:pc]
    new_ref, o_ref, m_ref, l_ref, acc_ref = refs[pc:]
    c = pl.program_id(1)

    @pl.when(c == 0)
    def _():
        _softmax_init(m_ref, l_ref, acc_ref)

    q = q_ref[0]
    kv_t = jnp.concatenate([r[...].astype(BF16) for r in page_refs], 1)
    chosen = _dot(sel_ref[0, 0], exp_ref[...]) > 0.5
    _softmax_step(_dot(q, kv_t[:NSA_HD]), chosen, NSA_HEADS, m_ref, l_ref, acc_ref, kv_t[NSA_HD:], vt=True)

    @pl.when(c == pl.num_programs(1) - 1)
    def _():
        kv_new = new_ref[0]
        tq = lax.broadcasted_iota(jnp.int32, (t_new, t_new), 0)
        tk = lax.broadcasted_iota(jnp.int32, (t_new, t_new), 1)
        _softmax_step(_dot_nt(q, kv_new[:, :NSA_HD]), tk <= tq, NSA_HEADS, m_ref, l_ref, acc_ref,
                      kv_new[:, NSA_HD:])
        o_ref[0] = _softmax_out(l_ref, acc_ref)


def _nsa_sel_sample(page_table, q, sel, cache_t, kv_new, l, t_new):
    db, rows, _ = q.shape
    n_pages = page_table.shape[1]
    page = cache_t.shape[3]
    pc = _page_chunk(n_pages)
    per = page // NSA_BLOCK
    nch = n_pages // pc
    sel_c = sel.reshape(db, t_new, nch, pc * per).transpose(0, 2, 1, 3)
    expand = (jnp.arange(pc * per)[:, None] == jnp.arange(pc * page)[None, :] // NSA_BLOCK).astype(BF16)

    def page_spec(k):
        return pl.BlockSpec((None, None, 2 * NSA_HD, page), lambda bi, c, pt: (l, pt[bi, c * pc + k], 1, 0))

    grid_spec = pltpu.PrefetchScalarGridSpec(
        num_scalar_prefetch=1,
        grid=(db, nch),
        in_specs=[pl.BlockSpec((1, rows, NSA_HD), lambda bi, c, pt: (bi, 0, 0)),
                  pl.BlockSpec((1, 1, t_new, pc * per), lambda bi, c, pt: (bi, c, 0, 0)),
                  pl.BlockSpec((pc * per, pc * page), lambda bi, c, pt: (0, 0))]
        + [page_spec(k) for k in range(pc)]
        + [pl.BlockSpec((1, t_new, LANE), lambda bi, c, pt: (bi, 0, 0))],
        out_specs=pl.BlockSpec((1, rows, NSA_HD), lambda bi, c, pt: (bi, 0, 0)),
        scratch_shapes=[pltpu.VMEM((rows, 1), F32), pltpu.VMEM((rows, 1), F32), pltpu.VMEM((rows, NSA_HD), F32)],
    )
    return pl.pallas_call(
        functools.partial(_nsa_sel_sample_body, pc=pc, t_new=t_new),
        grid_spec=grid_spec,
        out_shape=jax.ShapeDtypeStruct((db, rows, NSA_HD), F32),
        compiler_params=_cparams(("parallel", "arbitrary")),
        name="nsa_sel_sample",
    )(page_table, q, sel_c, expand, *([cache_t] * pc), kv_new)


def _nsa_win_sample_body(q_ref, buf_ref, new_ref, ocmp_ref, osel_ref, g_ref, o_ref, *, t_new):
    heads = NSA_HEADS
    lw = buf_ref.shape[2]
    q = q_ref[0]
    buf_t = buf_ref[0]
    kv_new = new_ref[0]
    rel_b = (lax.broadcasted_iota(jnp.int32, (t_new, lw), 0) + lw
             - lax.broadcasted_iota(jnp.int32, (t_new, lw), 1))
    mask_b = (rel_b >= 0) & (rel_b < NSA_WINDOW)
    mask_n = (lax.broadcasted_iota(jnp.int32, (t_new, t_new), 1)
              <= lax.broadcasted_iota(jnp.int32, (t_new, t_new), 0))
    s_b = _masked(_dot(q, buf_t[:NSA_HD]), mask_b, heads, NEG)
    s_n = _masked(_dot_nt(q, kv_new[:, :NSA_HD]), mask_n, heads, NEG)
    m = jnp.maximum(jnp.max(s_b, -1, keepdims=True), jnp.max(s_n, -1, keepdims=True))
    e_b = _masked(jnp.exp(s_b - m), mask_b, heads, 0.0)
    e_n = _masked(jnp.exp(s_n - m), mask_n, heads, 0.0)
    den = jnp.sum(e_b, -1, keepdims=True) + jnp.sum(e_n, -1, keepdims=True)
    o_win = (_dot_nt(e_b, buf_t[NSA_HD:]) + _dot(e_n, kv_new[:, NSA_HD:])) / jnp.where(den > 0, den, 1.0)
    o_cmp = ocmp_ref[0]
    o_sel = osel_ref[0]
    g = g_ref[0]
    for h in range(heads):
        rows = slice(h * t_new, (h + 1) * t_new)
        o_ref[0, :, h * NSA_HD:(h + 1) * NSA_HD] = (
            g[:, h:h + 1] * o_cmp[rows] + g[:, heads + h:heads + h + 1] * o_sel[rows]
            + g[:, 2 * heads + h:2 * heads + h + 1] * o_win[rows])


def _nsa_win_sample(q, buf_t, kv_new, o_cmp, o_sel, gates, t_new):
    db, rows, _ = q.shape
    lw = buf_t.shape[2]
    blk = lambda r, w: pl.BlockSpec((1, r, w), lambda bi: (bi, 0, 0))
    return pl.pallas_call(
        functools.partial(_nsa_win_sample_body, t_new=t_new),
        grid=(db,),
        in_specs=[blk(rows, NSA_HD), blk(LANE, lw), blk(t_new, LANE), blk(rows, NSA_HD), blk(rows, NSA_HD),
                  blk(t_new, LANE)],
        out_specs=blk(t_new, BRANCH_W),
        out_shape=jax.ShapeDtypeStruct((db, t_new, BRANCH_W), F32),
        compiler_params=_cparams(("parallel",)),
        name="nsa_win_sample",
    )(q, buf_t, kv_new, o_cmp, o_sel, gates)


def _rwkv_pre_body(zr_ref, zp_ref, mu_ref, w0_ref, a0_ref, wa_ref, g2_ref, kk_ref, ka_ref, ones_ref,
                   r_ref, k_ref, v_ref, lw_ref, kkn_ref, b_ref, g_ref):
    c = BRANCH_W
    zr = zr_ref[...]
    zs = zr + (zp_ref[...] - zr) * mu_ref[...]
    r, k, v = zs[:, :c], zs[:, c:2 * c], zs[:, 2 * c:3 * c]
    lora = zs[:, 3 * c:3 * c + LANE]
    lane = lax.broadcasted_iota(jnp.int32, lora.shape, 1)
    wa = _dot(jnp.where(lane < RWKV_W_LORA, jnp.tanh(lora), lora), wa_ref[...])
    x = -(w0_ref[...] + wa[:, :c])
    softplus = jnp.maximum(x, 0.0) + jnp.log(1.0 + jnp.exp(-jnp.abs(x)))
    a = jax.nn.sigmoid(a0_ref[...] + wa[:, c:])
    kk = k * kk_ref[...]
    norm = jnp.sqrt(_dot_split(kk * kk, ones_ref[...]))
    kk = kk / jnp.maximum(norm, 1e-12)
    r_ref[...] = r
    k_ref[...] = k * (1.0 + (a - 1.0) * ka_ref[...])
    v_ref[...] = v
    lw_ref[...] = -jnp.exp(-softplus - 0.5)
    kkn_ref[...] = kk
    b_ref[...] = kk * a
    g_ref[...] = _dot(jax.nn.sigmoid(zs[:, 3 * c + LANE:]), g2_ref[...])


def _rwkv_pre(zr, zprev, mu, w0, a0, wa, g2, k_k, k_a, ones_bd, l):
    n = zr.shape[0]
    tm = _pick_tile(n, (256, 128))
    c = BRANCH_W
    row = lambda w: pl.BlockSpec((tm, w), lambda i: (i, 0))
    vec = lambda w: _vec_spec(w, l, 1)
    return pl.pallas_call(
        _rwkv_pre_body,
        grid=(n // tm,),
        in_specs=[row(RWKV_IN), row(RWKV_IN), vec(RWKV_IN), vec(c), vec(c),
                  pl.BlockSpec((None, LANE, 2 * c), lambda i: (l, 0, 0)),
                  pl.BlockSpec((None, RWKV_G_LORA, c), lambda i: (l, 0, 0)),
                  vec(c), vec(c), pl.BlockSpec((c, c), lambda i: (0, 0))],
        out_specs=[row(c)] * 7,
        out_shape=[jax.ShapeDtypeStruct((n, c), F32)] * 7,
        compiler_params=_cparams(("parallel",)),
        name="rwkv_pre",
    )(zr, zprev, _vec3(mu), _vec3(w0), _vec3(a0), wa, g2, _vec3(k_k), _vec3(k_a), ones_bd)


def _rwkv_scan_body(r_ref, k_ref, v_ref, lw_ref, kk_ref, b_ref, s0_ref, y_ref, sf_ref, st_ref, *, c_len, nh):
    ci = pl.program_id(1)
    ch = nh * RWKV_HS
    rows = nh * c_len

    @pl.when(ci == 0)
    def _():
        st_ref[...] = s0_ref[0]

    lw = lw_ref[0]
    tri = (lax.broadcasted_iota(jnp.int32, (c_len, c_len), 0)
           >= lax.broadcasted_iota(jnp.int32, (c_len, c_len), 1)).astype(BF16)
    lp = None
    rem = lw
    for _ in range(3):
        piece = rem.astype(BF16)
        d = jnp.dot(tri, piece, preferred_element_type=F32)
        lp = d if lp is None else lp + d
        rem = rem - piece.astype(F32)
    p = jnp.exp(lp)
    p_inv = jnp.exp(-lp)
    hm = (lax.broadcasted_iota(jnp.int32, (rows, ch), 0) // c_len
          == lax.broadcasted_iota(jnp.int32, (rows, ch), 1) // RWKV_HS)

    def stack(a):
        return jnp.where(hm, jnp.concatenate([a] * nh, 0), 0.0).astype(BF16)

    x_kk = stack(kk_ref[0] * jnp.exp(lp - lw))
    x_r = stack(r_ref[0] * p)
    y_k = stack(k_ref[0] * p_inv)
    y_b = stack(b_ref[0] * p_inv)
    v = v_ref[0]
    vs = jnp.concatenate([v[:, h * RWKV_HS:(h + 1) * RWKV_HS] for h in range(nh)], 0)

    ti = lax.broadcasted_iota(jnp.int32, (rows, rows), 0) % c_len
    tj = lax.broadcasted_iota(jnp.int32, (rows, rows), 1) % c_len
    strict, incl = ti > tj, ti >= tj
    n_mat = jnp.where(strict, -_dot_nt(x_kk, y_b), 0.0)
    a_kk = jnp.where(strict, _dot_nt(x_kk, y_k), 0.0)
    a_rk = jnp.where(incl, _dot_nt(x_r, y_k), 0.0)
    a_rb = jnp.where(incl, _dot_nt(x_r, y_b), 0.0)
    eye = jnp.where(lax.broadcasted_iota(jnp.int32, (rows, rows), 0)
                    == lax.broadcasted_iota(jnp.int32, (rows, rows), 1), 1.0, 0.0)
    t_inv = eye + n_mat
    pw = n_mat
    span = 2
    while span < c_len:
        pw = _dot(pw, pw)
        t_inv = t_inv + _dot(t_inv, pw)
        span *= 2

    st = st_ref[...]
    u = _dot(t_inv, _dot_nt(x_kk, st) + _dot(a_kk, vs))
    y = _dot_nt(x_r, st) + _dot(a_rk, vs) - _dot(a_rb, u)
    st_new = p[c_len - 1:c_len, :] * (st + _dot(vs.T, y_k) - _dot(u.T, y_b))
    st_ref[...] = st_new
    for h in range(nh):
        y_ref[0, :, h * RWKV_HS:(h + 1) * RWKV_HS] = y[h * c_len:(h + 1) * c_len]

    @pl.when(ci == pl.num_programs(1) - 1)
    def _():
        sf_ref[0] = st_new


def _rwkv_scan(r, k, v, lw, kk, b, s0, c_len):
    nseq, t, ch = r.shape
    nh = ch // RWKV_HS
    assert nh * c_len == SCAN_ROWS
    blk = pl.BlockSpec((1, c_len, ch), lambda si, ci: (si, ci, 0))
    st = pl.BlockSpec((1, RWKV_HS, ch), lambda si, ci: (si, 0, 0))
    return pl.pallas_call(
        functools.partial(_rwkv_scan_body, c_len=c_len, nh=nh),
        grid=(nseq, t // c_len),
        in_specs=[blk] * 6 + [st],
        out_specs=[blk, st],
        out_shape=[jax.ShapeDtypeStruct((nseq, t, ch), F32), jax.ShapeDtypeStruct((nseq, RWKV_HS, ch), F32)],
        scratch_shapes=[pltpu.VMEM((RWKV_HS, ch), F32)],
        compiler_params=_cparams(("parallel", "arbitrary")),
        name="rwkv_scan",
    )(r, k, v, lw, kk, b, s0)


def _rwkv_post_body(y_ref, r_ref, k_ref, v_ref, g_ref, rk_ref, lnw_ref, lnb_ref, ones_ref, o_ref):
    y = y_ref[...]
    ones = ones_ref[...]
    mu = _dot_split(y, ones) * (1.0 / RWKV_HS)
    dev = y - mu
    var = _dot_split(dev * dev, ones) * (1.0 / RWKV_HS)
    yn = dev * lax.rsqrt(var + RWKV_GN_EPS) * lnw_ref[...] + lnb_ref[...]
    bonus = _dot_split(r_ref[...] * k_ref[...] * rk_ref[...], ones) * v_ref[...]
    o_ref[...] = (yn + bonus) * g_ref[...]


def _rwkv_post(y, r, k, v, g, r_k, ln_w, ln_b, ones_bd, l):
    n, c = y.shape
    tm = _pick_tile(n, (256, 128))
    row = pl.BlockSpec((tm, c), lambda i: (i, 0))
    vec = _vec_spec(c, l, 1)
    return pl.pallas_call(
        _rwkv_post_body,
        grid=(n // tm,),
        in_specs=[row] * 5 + [vec] * 3 + [pl.BlockSpec((c, c), lambda i: (0, 0))],
        out_specs=row,
        out_shape=jax.ShapeDtypeStruct((n, c), F32),
        compiler_params=_cparams(("parallel",)),
        name="rwkv_post",
    )(y, r, k, v, g, _vec3(r_k), _vec3(ln_w), _vec3(ln_b), ones_bd)


def _rope_tables(pos):
    inv = ROPE_THETA ** (-jnp.arange(0, NSA_HD, 2, dtype=F32) / NSA_HD)
    ang = pos.astype(F32)[:, None] * inv[None, :]
    c, s = jnp.cos(ang), jnp.sin(ang)
    return jnp.concatenate([c, c], -1), jnp.concatenate([-s, s], -1)


def _table(cos, sin, groups):
    n = cos.shape[0]
    cs = [cos if rot else jnp.ones((n, w), F32) for w, rot in groups]
    ss = [sin if rot else jnp.zeros((n, w), F32) for w, rot in groups]
    return jnp.concatenate(cs, -1), jnp.concatenate(ss, -1)


def _heads_first(x, tile, group, t_new):
    nt, h, _, d = x.shape
    x = x.reshape(nt, h, tile // t_new, t_new, d).transpose(0, 2, 1, 3, 4)
    return x.reshape(group, h * t_new, d)


def kernel(x_prompt, x_sample, cache_mla, cache_nsa, state_nsa_win, state_pool, state_rwkv, state_rwkv_shift,
           page_table, norm_ffn1, ffn1_w_gate, ffn1_w_up, ffn1_w_down, norm_mix, w_in, w_branch, w_out, pool_w,
           pool_scale, mla_q_norm, mla_w_uq, mla_kv_norm, mla_w_uk, mla_w_uv, nsa_phi_k, nsa_phi_v, rwkv_mu,
           rwkv_w0, rwkv_w2, rwkv_a0, rwkv_a2, rwkv_g2, rwkv_k_k, rwkv_k_a, rwkv_r_k, rwkv_ln_w, rwkv_ln_b,
           norm_ffn2, ffn2_w_gate, ffn2_w_up, ffn2_w_down, norm_final):
    b, s, d = x_prompt.shape
    db, t_new = x_sample.shape[:2]
    depth = norm_ffn1.shape[0]
    n_p, n_s = b * s, db * t_new
    n_pages, page = page_table.shape[1], cache_mla.shape[2]
    past = n_pages * page
    win_len = state_nsa_win.shape[2]
    n_phys = cache_mla.shape[1]
    assert d == D_MODEL and s % Q_TILE == 0 and n_s % Q_TILE == 0 and Q_TILE % t_new == 0
    assert s >= win_len and page % NSA_BLOCK == 0 and t_new < NSA_BLOCK and t_new <= POOL_HALO
    samp_c = t_new
    samp_grp = SCAN_ROWS // (samp_c * RWKV_HEADS)
    assert db % samp_grp == 0 and s % 32 == 0

    bf = lambda w: w.astype(BF16)
    f1 = (bf(ffn1_w_gate), bf(ffn1_w_up), bf(ffn1_w_down))
    f2 = (bf(ffn2_w_gate), bf(ffn2_w_up), bf(ffn2_w_down))
    o_mla = BRANCH_W
    o_nsa = o_mla + MLA_Q_LORA + MLA_CACHE_W
    o_rw = o_nsa + BRANCH_W + 6 * NSA_HD + 3 * NSA_HEADS
    o_gate = o_rw + RWKV_IN
    zpad = lambda w: jnp.zeros((depth, d, w), w_in.dtype)
    w_in_p = bf(jnp.concatenate([
        w_in[:, :, o_gate:], w_in[:, :, :o_mla], w_in[:, :, o_mla:o_mla + MLA_Q_LORA],
        w_in[:, :, o_mla + MLA_Q_LORA:o_nsa], zpad(MLA_ROW_W - MLA_CACHE_W),
        w_in[:, :, o_nsa:o_nsa + BRANCH_W], w_in[:, :, o_nsa + BRANCH_W:o_nsa + BRANCH_W + 6 * NSA_HD],
        w_in[:, :, o_nsa + BRANCH_W + 6 * NSA_HD:o_rw], zpad(LANE - 3 * NSA_HEADS),
        w_in[:, :, o_rw:o_gate], zpad(Z_W - Z_END)], -1))
    wb = bf(w_branch)
    wo = bf(w_out)
    pw = bf(pool_w)
    uq = mla_w_uq.reshape(depth, MLA_Q_LORA, MLA_HEADS, MLA_NOPE + MLA_ROPE)
    wuq = bf(jnp.concatenate([uq[..., :MLA_NOPE].reshape(depth, MLA_Q_LORA, -1),
                              uq[..., MLA_NOPE:].reshape(depth, MLA_Q_LORA, -1)], -1))
    wuk = bf(mla_w_uk.transpose(0, 2, 3, 1))
    wuv = bf(mla_w_uv.transpose(0, 2, 1, 3))
    zz = jnp.zeros((depth, NSA_HD, NSA_HD), F32)
    phi = bf(jnp.concatenate([jnp.concatenate([nsa_phi_k, zz], -1), jnp.concatenate([zz, nsa_phi_v], -1)], 1))
    phi_t = phi.transpose(0, 2, 1)
    zw =jnp.zeros((depth, RWKV_W_LORA, BRANCH_W), F32)
    wa = bf(jnp.concatenate([jnp.concatenate([rwkv_w2, zw], -1), jnp.concatenate([zw, rwkv_a2], -1)], 1))
    g2 = bf(rwkv_g2)
    r_k = rwkv_r_k.reshape(depth, BRANCH_W)
    ones_bd = (jnp.arange(BRANCH_W)[:, None] // RWKV_HS == jnp.arange(BRANCH_W)[None, :] // RWKV_HS).astype(BF16)

    pos = jnp.concatenate([jnp.tile(jnp.arange(s), b), jnp.tile(past + jnp.arange(t_new), db)])
    cos, sin = _rope_tables(pos)
    tab_mq = _table(cos, sin, [(MLA_HEADS * MLA_NOPE, False)] + [(MLA_ROPE, True)] * MLA_HEADS)
    tab_mkv = _table(cos, sin, [(MLA_KV_LORA, False), (MLA_ROPE, True), (MLA_ROW_W - MLA_CACHE_W, False)])
    tab_nq = _table(cos, sin, [(NSA_HD, True)] * NSA_HEADS)
    tab_nkv = _table(cos, sin, [(2 * NSA_HD, False), (NSA_HD, True), (NSA_HD, False), (NSA_HD, True),
                                (NSA_HD, False)])
    nb_p = s // NSA_BLOCK
    nb_s = past // NSA_BLOCK
    tab_bp = _table(*_rope_tables(jnp.arange(nb_p) * NSA_BLOCK + NSA_BLOCK - 1), [(NSA_HD, True), (NSA_HD, False)])
    tab_bs = [t.T for t in _rope_tables(jnp.arange(nb_s) * NSA_BLOCK + NSA_BLOCK - 1)]

    cache_mla_t = cache_mla.transpose(0, 1, 3, 2)
    cache_nsa_t = cache_nsa.transpose(0, 1, 3, 4, 2).reshape(depth, n_phys, 4 * NSA_HD, page)
    x = jnp.concatenate([x_prompt.reshape(n_p, d), x_sample.reshape(n_s, d)], 0)
    ns_tiles = n_s // Q_TILE
    new_p = [[] for _ in range(6)]
    new_s = [[] for _ in range(6)]
    for l in range(depth):
        x = _ffn(x, norm_ffn1[l], *f1, norm_final, l, False)
        z = _inproj(x, norm_mix[l], w_in_p, l)

        zp = z[:, Z_POOL:Z_POOL + BRANCH_W]
        zp_p = zp[:n_p].reshape(b, s, BRANCH_W)
        zp_s = zp[n_p:].reshape(db, t_new, BRANCH_W)
        pre_p = jnp.zeros((b, POOL_HALO, BRANCH_W), F32)
        pre_s = jnp.concatenate([jnp.zeros((db, POOL_HALO - POOL_BUF, BRANCH_W), F32), state_pool[l]], 1)
        o_pool_p = _pool(jnp.concatenate([pre_p, zp_p], 1), pw, pool_scale, l, 0, 1)
        o_pool_s = _pool(jnp.concatenate([pre_s, zp_s], 1), pw, pool_scale, l, past, _pick_tile(db, (32, 16, 8, 1)))
        o_pool = jnp.concatenate([o_pool_p.reshape(n_p, BRANCH_W), o_pool_s.reshape(n_s, BRANCH_W)], 0)
        new_p[3].append(zp_p[:, -POOL_BUF:])
        new_s[3].append(jnp.concatenate([state_pool[l], zp_s], 1)[:, -POOL_BUF:])

        qcat, qcat_t, rows, rows_b, rows_t = _mla_proj(
            z[:, Z_MQ:Z_MQ + MLA_Q_LORA], z[:, Z_MKV:Z_MKV + MLA_ROW_W], mla_q_norm, mla_kv_norm, wuq, wuk,
            tab_mq + tab_mkv, l)
        o_mla_p = _mla_prompt(qcat_t, rows_b[:n_p], rows_t, wuv, l, b, s)
        q_s = _heads_first(qcat[n_p // Q_TILE:], Q_TILE, db, t_new)
        o_mla_s = _mla_sample(page_table, q_s, cache_mla_t, rows_b[n_p:].reshape(db, t_new, MLA_ROW_W), wuv, l)
        o_mla_all = jnp.concatenate([o_mla_p, o_mla_s.reshape(n_s, BRANCH_W)], 0)
        new_p[0].append(rows[:n_p, :MLA_CACHE_W].reshape(b, s, MLA_CACHE_W))
        new_s[0].append(rows[n_p:, :MLA_CACHE_W].reshape(db, t_new, MLA_CACHE_W))

        nq, nq_t, nkv, ng = _nsa_proj(z[:, Z_NQ:Z_NQ + BRANCH_W], z[:, Z_NKV:Z_NKV + 6 * NSA_HD],
                                      z[:, Z_NG:Z_NG + LANE], tab_nq + tab_nkv)
        kbvb_p = _cmp_proj(_bmean_prompt(nkv[:n_p], b, s), phi, *tab_bp, l)
        o_nsa_p = _nsa_prompt(nq_t, kbvb_p, nkv[:n_p], ng[:n_p], b, s)
        nq_s = _heads_first(nq[n_p // Q_TILE:], Q_TILE, db, t_new)
        nkv_s = nkv[n_p:].reshape(db, t_new, 6 * NSA_HD)
        kbvb_s = _cmp_proj_t(_bmean_sample(page_table, cache_nsa_t, l), phi_t, *tab_bs, l)
        o_cmp_s, sel_s = _nsa_cmp_sample(nq_s, kbvb_s, t_new)
        o_sel_s = _nsa_sel_sample(page_table, nq_s, sel_s, cache_nsa_t, nkv_s[:, :, 2 * NSA_HD:4 * NSA_HD], l, t_new)
        win_buf_t = state_nsa_win[l].transpose(0, 2, 3, 1).reshape(db, 2 * NSA_HD, win_len)
        o_nsa_s = _nsa_win_sample(nq_s, win_buf_t, nkv_s[:, :, 4 * NSA_HD:], o_cmp_s, o_sel_s,
                                  ng[n_p:].reshape(db, t_new, LANE), t_new)
        o_nsa_all = jnp.concatenate([o_nsa_p, o_nsa_s.reshape(n_s, BRANCH_W)], 0)
        nkv_p = nkv[:n_p].reshape(b, s, 6, NSA_HD)
        new_p[1].append(nkv_p[:, :, :4])
        new_s[1].append(nkv_s[:, :, :4 * NSA_HD].reshape(db, t_new, 4, NSA_HD))
        new_p[2].append(nkv_p[:, s - win_len:, 4:])
        new_s[2].append(jnp.concatenate([state_nsa_win[l], nkv_s[:, :, 4 * NSA_HD:].reshape(db, t_new, 2, NSA_HD)],
                                        1)[:, -win_len:])

        zr = z[:, Z_RW:Z_RW + RWKV_IN]
        zr_p = zr[:n_p].reshape(b, s, RWKV_IN)
        zr_s = zr[n_p:].reshape(db, t_new, RWKV_IN)
        zprev = jnp.concatenate([
            jnp.concatenate([jnp.zeros((b, 1, RWKV_IN), F32), zr_p[:, :-1]], 1).reshape(n_p, RWKV_IN),
            jnp.concatenate([state_rwkv_shift[l][:, None], zr_s[:, :-1]], 1).reshape(n_s, RWKV_IN)], 0)
        pre = _rwkv_pre(zr, zprev, rwkv_mu, rwkv_w0, rwkv_a0, wa, g2, rwkv_k_k, rwkv_k_a, ones_bd, l)
        r_, k_, v_, lw_, kk_, b_, g_ = pre
        seq_p = lambda a: a[:n_p].reshape(b, s, BRANCH_W)
        seq_s = lambda a: (a[n_p:].reshape(db // samp_grp, samp_grp, t_new, BRANCH_W).transpose(0, 2, 1, 3)
                           .reshape(db // samp_grp, t_new, samp_grp * BRANCH_W))
        scan_in = (r_, k_, v_, lw_, kk_, b_)
        y_p, sf_p = _rwkv_scan(*[seq_p(a) for a in scan_in],
                               jnp.zeros((b, RWKV_HS, BRANCH_W), F32), SCAN_ROWS // RWKV_HEADS)
        s0_s = (state_rwkv[l].reshape(db // samp_grp, samp_grp, RWKV_HEADS, RWKV_HS, RWKV_HS)
                .transpose(0, 3, 1, 2, 4).reshape(db // samp_grp, RWKV_HS, samp_grp * BRANCH_W))
        y_s, sf_s = _rwkv_scan(*[seq_s(a) for a in scan_in], s0_s, samp_c)
        y_s = (y_s.reshape(db // samp_grp, t_new, samp_grp, BRANCH_W).transpose(0, 2, 1, 3)
               .reshape(n_s, BRANCH_W))
        y_all = jnp.concatenate([y_p.reshape(n_p, BRANCH_W), y_s], 0)
        o_rwkv = _rwkv_post(y_all, r_, k_, v_, g_, r_k, rwkv_ln_w, rwkv_ln_b, ones_bd, l)
        new_p[4].append(sf_p.reshape(b, RWKV_HS, RWKV_HEADS, RWKV_HS).transpose(0, 2, 1, 3))
        new_s[4].append(sf_s.reshape(db // samp_grp, RWKV_HS, samp_grp, RWKV_HEADS, RWKV_HS)
                        .transpose(0, 2, 3, 1, 4).reshape(db, RWKV_HEADS, RWKV_HS, RWKV_HS))
        new_p[5].append(zr_p[:, -1])
        new_s[5].append(zr_s[:, -1])

        branches = jnp.concatenate([o_pool, o_mla_all, o_nsa_all, o_rwkv], -1)
        x = _merge(x, branches, z, wb, wo, l)
        x = _ffn(x, norm_ffn2[l], *f2, norm_final, l, l == depth - 1)

    y_prompt = x[:n_p].reshape(b, s, d)
    y_sample = x[n_p:].reshape(db, t_new, d)
    outs_p = [jnp.stack(a, 0) for a in new_p]
    outs_s = [jnp.stack(a, 0) for a in new_s]
    res = [y_prompt, y_sample]
    for a, c in zip(outs_p, outs_s):
        res += [a, c]
    return tuple(res)
```

```python
import functools

import jax
import jax.numpy as jnp
from jax import lax
from jax.experimental import pallas as pl
from jax.experimental.pallas import tpu as pltpu

F32 = jnp.float32
BF16 = jnp.bfloat16

D_MODEL = 2048
N_BRANCH = 4
BRANCH_W = D_MODEL // N_BRANCH
ROPE_THETA = 10000.0
RMS_EPS = 1e-6
NEG = -1e30
POOL_WINDOWS = (2, 4, 8, 16)
POOL_GW = BRANCH_W // len(POOL_WINDOWS)
POOL_BUF = max(POOL_WINDOWS) - 1
POOL_HALO = 16
MLA_HEADS = 4
MLA_NOPE = 128
MLA_ROPE = 64
MLA_V = BRANCH_W // MLA_HEADS
MLA_Q_LORA = D_MODEL // 4
MLA_KV_LORA = D_MODEL // 8
MLA_CACHE_W = MLA_KV_LORA + MLA_ROPE
MLA_ROW_W = 384
MLA_SCALE = (MLA_NOPE + MLA_ROPE) ** -0.5
NSA_HEADS = 8
NSA_HD = BRANCH_W // NSA_HEADS
NSA_BLOCK = 64
NSA_TOPN = 16
NSA_WINDOW = 512
NSA_SCALE = NSA_HD ** -0.5
RWKV_HS = 64
RWKV_HEADS = BRANCH_W // RWKV_HS
RWKV_W_LORA = 64
RWKV_A_LORA = 64
RWKV_G_LORA = 128
RWKV_GN_EPS = 64e-5
RWKV_IN = 3 * BRANCH_W + RWKV_W_LORA + RWKV_A_LORA + RWKV_G_LORA
SCAN_ROWS = 256
Q_TILE = 128
PAGE_GROUP = 64
LANE = 128

Z_GATE = 0
Z_POOL = N_BRANCH * D_MODEL
Z_MQ = Z_POOL + BRANCH_W
Z_MKV = Z_MQ + MLA_Q_LORA
Z_NQ = Z_MKV + MLA_ROW_W
Z_NKV = Z_NQ + BRANCH_W
Z_NG = Z_NKV + 6 * NSA_HD
Z_RW = Z_NG + LANE
Z_END = Z_RW + RWKV_IN
Z_TILE = 512
Z_W = -(-Z_END // Z_TILE) * Z_TILE


def _cparams(sem, vmem_mb=48):
    return pltpu.CompilerParams(dimension_semantics=sem, vmem_limit_bytes=vmem_mb * 1024 * 1024)


def _pick_tile(n, cands):
    for c in cands:
        if n % c == 0:
            return c
    raise ValueError(f"no tile in {cands} divides {n}")


def _vec3(a):
    return a.reshape(a.shape[0], 1, a.shape[1])


def _vec_spec(w, l, ngrid):
    if ngrid == 1:
        return pl.BlockSpec((None, 1, w), lambda i: (l, 0, 0))
    return pl.BlockSpec((None, 1, w), lambda i, j: (l, 0, 0))


def _dot(a, b):
    return jnp.dot(a.astype(BF16), b.astype(BF16), preferred_element_type=F32)


def _dot_nt(a, b):
    return lax.dot_general(a.astype(BF16), b.astype(BF16), (((1,), (1,)), ((), ())),
                           preferred_element_type=F32)


def _dot_split(a, b01, terms=2):
    acc = None
    rem = a
    for _ in range(terms):
        piece = rem.astype(BF16)
        d = jnp.dot(piece, b01, preferred_element_type=F32)
        acc = d if acc is None else acc + d
        rem = rem - piece.astype(F32)
    return acc


def _rms(x, g):
    return x * lax.rsqrt(jnp.mean(x * x, -1, keepdims=True) + RMS_EPS) * g


def _rope_apply(x, cos, sin):
    w = x.shape[-1]
    lane = lax.broadcasted_iota(jnp.int32, x.shape, x.ndim - 1)
    fwd = pltpu.roll(x, w - 32, x.ndim - 1)
    bwd = pltpu.roll(x, 32, x.ndim - 1)
    return x * cos + jnp.where((lane & 63) < 32, fwd, bwd) * sin


def _ffn_body(x_ref, g_ref, wg_ref, wu_ref, wd_ref, gf_ref, o_ref, h_ref, acc_ref, *, nj, final_norm):
    j = pl.program_id(1)

    @pl.when(j == 0)
    def _():
        h_ref[...] = _rms(x_ref[...], g_ref[...]).astype(BF16)
        acc_ref[...] = jnp.zeros(acc_ref.shape, F32)

    half = x_ref.shape[0] // 2
    for r in (pl.ds(0, half), pl.ds(half, half)):
        h = h_ref[r]
        gt = jnp.dot(h, wg_ref[...], preferred_element_type=F32)
        up = jnp.dot(h, wu_ref[...], preferred_element_type=F32)
        acc_ref[r] += _dot(gt * jax.nn.sigmoid(gt) * up, wd_ref[...])

    @pl.when(j == nj - 1)
    def _():
        y = x_ref[...] + 0.5 * acc_ref[...]
        if final_norm:
            y = _rms(y, gf_ref[...])
        o_ref[...] = y


def _col_tiles(w, t):
    layers, d, f = w.shape
    return w.reshape(layers, d, f // t, t).transpose(0, 2, 1, 3)


def _ffn(x, g, wg, wu, wd, gf, l, final_norm):
    n, d = x.shape
    nj, tf = wg.shape[1], wg.shape[3]
    tm = _pick_tile(n, (512, 256, 128))
    return pl.pallas_call(
        functools.partial(_ffn_body, nj=nj, final_norm=final_norm),
        grid=(n // tm, nj),
        in_specs=[
            pl.BlockSpec((tm, d), lambda i, j: (i, 0)),
            pl.BlockSpec((1, d), lambda i, j: (0, 0)),
            pl.BlockSpec((None, None, d, tf), lambda i, j: (l, j, 0, 0)),
            pl.BlockSpec((None, None, d, tf), lambda i, j: (l, j, 0, 0)),
            pl.BlockSpec((None, tf, d), lambda i, j: (l, j, 0)),
            pl.BlockSpec((1, d), lambda i, j: (0, 0)),
        ],
        out_specs=pl.BlockSpec((tm, d), lambda i, j: (i, 0)),
        out_shape=jax.ShapeDtypeStruct((n, d), F32),
        scratch_shapes=[pltpu.VMEM((tm, d), BF16), pltpu.VMEM((tm, d), F32)],
        compiler_params=_cparams(("parallel", "arbitrary")),
        name="ffn",
    )(x, g.reshape(1, d), wg, wu, wd, gf.reshape(1, d))


def _inproj_body(x_ref, g_ref, w_ref, o_ref, h_ref):
    @pl.when(pl.program_id(1) == 0)
    def _():
        h_ref[...] = _rms(x_ref[...], g_ref[...]).astype(BF16)

    o_ref[...] = jnp.dot(h_ref[...], w_ref[...], preferred_element_type=F32)


def _inproj(x, g, w, l):
    n, d = x.shape
    zw = w.shape[1] * Z_TILE
    tm = _pick_tile(n, (1024, 512, 256, 128))
    return pl.pallas_call(
        _inproj_body,
        grid=(n // tm, zw // Z_TILE),
        in_specs=[
            pl.BlockSpec((tm, d), lambda i, j: (i, 0)),
            pl.BlockSpec((1, d), lambda i, j: (0, 0)),
            pl.BlockSpec((None, None, d, Z_TILE), lambda i, j: (l, j, 0, 0)),
        ],
        out_specs=pl.BlockSpec((tm, Z_TILE), lambda i, j: (i, j)),
        out_shape=jax.ShapeDtypeStruct((n, zw), F32),
        scratch_shapes=[pltpu.VMEM((tm, d), BF16)],
        compiler_params=_cparams(("parallel", "arbitrary")),
        name="inproj",
    )(x, g.reshape(1, d), w)


def _merge_body(x_ref, br_ref, zg_ref, wb_ref, wo_ref, o_ref, acc_ref):
    n = pl.program_id(1)
    t = jax.nn.sigmoid(zg_ref[...]) * _dot(br_ref[...], wb_ref[...])

    @pl.when(n == 0)
    def _():
        acc_ref[...] = t

    @pl.when(n > 0)
    def _():
        acc_ref[...] += t

    @pl.when(n == N_BRANCH - 1)
    def _():
        o_ref[...] = x_ref[...] + _dot(acc_ref[...], wo_ref[...])


def _merge(x, branches, z, wb, wo, l):
    n, d = x.shape
    tm = _pick_tile(n, (256, 128))
    return pl.pallas_call(
        _merge_body,
        grid=(n // tm, N_BRANCH),
        in_specs=[
            pl.BlockSpec((tm, d), lambda i, k: (i, 0)),
            pl.BlockSpec((tm, BRANCH_W), lambda i, k: (i, k)),
            pl.BlockSpec((tm, d), lambda i, k: (i, k)),
            pl.BlockSpec((None, None, BRANCH_W, d), lambda i, k: (l, k, 0, 0)),
            pl.BlockSpec((None, d, d), lambda i, k: (l, 0, 0)),
        ],
        out_specs=pl.BlockSpec((tm, d), lambda i, k: (i, 0)),
        out_shape=jax.ShapeDtypeStruct((n, d), F32),
        scratch_shapes=[pltpu.VMEM((tm, d), F32)],
        compiler_params=_cparams(("parallel", "arbitrary")),
        name="merge",
    )(x, branches, z, wb, wo)


def _pool_body(u_ref, w_ref, sc_ref, o_ref, *, t_len, t_chunk, pos0):
    g = pl.program_id(1)
    bt = u_ref.shape[0]
    for gi, win in enumerate(POOL_WINDOWS):
        @pl.when(g == gi)
        def _(win=win):
            for c0 in range(0, t_len, t_chunk):
                ue = u_ref[:, c0:c0 + t_chunk + POOL_HALO, :]
                acc = ue
                span = 1
                while span < win:
                    acc = acc[:, span:, :] + acc[:, :-span, :]
                    span *= 2
                wsum = acc[:, POOL_HALO + 1 - win:POOL_HALO + 1 - win + t_chunk, :]
                pos = pos0 + c0 + lax.broadcasted_iota(jnp.int32, (1, t_chunk, 1), 1)
                cnt = jnp.minimum(win, pos + 1).astype(F32)
                diff = wsum / cnt - ue[:, POOL_HALO:, :]
                y = _dot(diff.reshape(bt * t_chunk, LANE), w_ref[...]) * sc_ref[...]
                o_ref[:, c0:c0 + t_chunk, :] = y.reshape(bt, t_chunk, LANE)


def _pool(u_ext, w, scale, l, pos0, bt):
    b, te, c = u_ext.shape
    t_len = te - POOL_HALO
    t_chunk = _pick_tile(t_len, (512, 256, 128, 8))
    return pl.pallas_call(
        functools.partial(_pool_body, t_len=t_len, t_chunk=t_chunk, pos0=pos0),
        grid=(b // bt, len(POOL_WINDOWS)),
        in_specs=[
            pl.BlockSpec((bt, te, LANE), lambda i, g: (i, 0, g)),
            pl.BlockSpec((None, None, POOL_GW, POOL_GW), lambda i, g: (l, g, 0, 0)),
            pl.BlockSpec((None, 1, LANE), lambda i, g: (l, 0, g)),
        ],
        out_specs=pl.BlockSpec((bt, t_len, LANE), lambda i, g: (i, 0, g)),
        out_shape=jax.ShapeDtypeStruct((b, t_len, c), F32),
        compiler_params=_cparams(("parallel", "arbitrary")),
        name="pool",
    )(u_ext, w, _vec3(scale))


def _mla_proj_body(zq_ref, zkv_ref, qn_ref, kvn_ref, wuq_ref, wuk_ref, cq_ref, sq_ref, ckv_ref, skv_ref,
                   qcat_ref, qcat_t_ref, rows_ref, rowsb_ref, rows_t_ref):
    tm = zq_ref.shape[0]
    zk = zkv_ref[...]
    ckv = _rms(zk[:, :MLA_KV_LORA], kvn_ref[...])
    roped = _rope_apply(zk, ckv_ref[...], skv_ref[...])
    rows = jnp.concatenate([ckv, roped[:, MLA_KV_LORA:]], -1)
    rows_ref[...] = rows
    rowsb_ref[...] = rows.astype(BF16)
    rows_t_ref[0] = rows.T.astype(BF16)
    q = _dot(_rms(zq_ref[...], qn_ref[...]), wuq_ref[...])
    q = _rope_apply(q, cq_ref[...], sq_ref[...])
    pad = jnp.zeros((tm, MLA_ROW_W - MLA_CACHE_W), F32)
    for h in range(MLA_HEADS):
        q_lat = _dot(q[:, h * MLA_NOPE:(h + 1) * MLA_NOPE], wuk_ref[h])
        q_pe = q[:, MLA_HEADS * MLA_NOPE + h * MLA_ROPE:MLA_HEADS * MLA_NOPE + (h + 1) * MLA_ROPE]
        qcat = jnp.concatenate([q_lat, q_pe, pad], -1) * MLA_SCALE
        qcat_ref[0, h] = qcat.astype(BF16)
        qcat_t_ref[0, :, h * tm:(h + 1) * tm] = qcat.T.astype(BF16)


def _mla_proj(zq, zkv, qn, kvn, wuq, wuk, tabs, l):
    n = zq.shape[0]
    tm = Q_TILE
    qw = wuq.shape[2]
    row = lambda w: pl.BlockSpec((tm, w), lambda i: (i, 0))
    return pl.pallas_call(
        _mla_proj_body,
        grid=(n // tm,),
        in_specs=[
            row(MLA_Q_LORA), row(MLA_ROW_W),
            _vec_spec(MLA_Q_LORA, l, 1), _vec_spec(MLA_KV_LORA, l, 1),
            pl.BlockSpec((None, MLA_Q_LORA, qw), lambda i: (l, 0, 0)),
            pl.BlockSpec((None, MLA_HEADS, MLA_NOPE, MLA_KV_LORA), lambda i: (l, 0, 0, 0)),
            row(qw), row(qw), row(MLA_ROW_W), row(MLA_ROW_W),
        ],
        out_specs=[
            pl.BlockSpec((1, MLA_HEADS, tm, MLA_ROW_W), lambda i: (i, 0, 0, 0)),
            pl.BlockSpec((1, MLA_ROW_W, MLA_HEADS * tm), lambda i: (i, 0, 0)),
            row(MLA_ROW_W), row(MLA_ROW_W),
            pl.BlockSpec((1, MLA_ROW_W, tm), lambda i: (i, 0, 0)),
        ],
        out_shape=[
            jax.ShapeDtypeStruct((n // tm, MLA_HEADS, tm, MLA_ROW_W), BF16),
            jax.ShapeDtypeStruct((n // tm, MLA_ROW_W, MLA_HEADS * tm), BF16),
            jax.ShapeDtypeStruct((n, MLA_ROW_W), F32),
            jax.ShapeDtypeStruct((n, MLA_ROW_W), BF16),
            jax.ShapeDtypeStruct((n // tm, MLA_ROW_W, tm), BF16),
        ],
        compiler_params=_cparams(("parallel",)),
        name="mla_proj",
    )(zq, zkv, _vec3(qn), _vec3(kvn), wuq, wuk, *tabs)


def _masked(x, mask, heads, fill):
    if mask is None:
        return x
    tq, tk = mask.shape
    return jnp.where(mask[None], x.reshape(heads, tq, tk), fill).reshape(heads * tq, tk)


def _softmax_update(state, s, mask, heads, v, vt=False):
    m_old, l_old, acc_old = state
    s = _masked(s, mask, heads, NEG)
    m_new = jnp.maximum(m_old, jnp.max(s, -1, keepdims=True))
    alpha = jnp.exp(m_old - m_new)
    p = _masked(jnp.exp(s - m_new), mask, heads, 0.0)
    return (m_new, alpha * l_old + jnp.sum(p, -1, keepdims=True),
            alpha * acc_old + (_dot_nt(p, v) if vt else _dot(p, v)))


def _softmax_step(s, mask, heads, m_ref, l_ref, acc_ref, v, vt=False):
    m_ref[...], l_ref[...], acc_ref[...] = _softmax_update((m_ref[...], l_ref[...], acc_ref[...]), s, mask, heads,
                                                           v, vt)


def _softmax_once(s, mask, heads, v, vt=False):
    s = _masked(s, mask, heads, NEG)
    e = _masked(jnp.exp(s - jnp.max(s, -1, keepdims=True)), mask, heads, 0.0)
    den = jnp.sum(e, -1, keepdims=True)
    p = e / jnp.where(den > 0, den, 1.0)
    return (_dot_nt(p, v) if vt else _dot(p, v)), p


def _lanes(a, heads):
    return jnp.concatenate([a] * heads, 1)


def _softmax_step_t(st, keep, m_ref, l_ref, acc_ref, vt):
    if keep is not None:
        keep = keep > 0.5
        st = jnp.where(keep, st, NEG)
    m_old = m_ref[...]
    m_new = jnp.maximum(m_old, jnp.max(st, 0, keepdims=True))
    alpha = jnp.exp(m_old - m_new)
    p = jnp.exp(st - m_new)
    if keep is not None:
        p = jnp.where(keep, p, 0.0)
    l_ref[...] = alpha * l_ref[...] + jnp.sum(p, 0, keepdims=True)
    acc_ref[...] = alpha * acc_ref[...] + _dot(vt, p)
    m_ref[...] = m_new


def _softmax_once_t(st, keep, vt):
    keep = keep > 0.5
    st = jnp.where(keep, st, NEG)
    e = jnp.where(keep, jnp.exp(st - jnp.max(st, 0, keepdims=True)), 0.0)
    den = jnp.sum(e, 0, keepdims=True)
    p = e / jnp.where(den > 0, den, 1.0)
    return _dot(vt, p), p


def _softmax_init(m_ref, l_ref, acc_ref):
    m_ref[...] = jnp.full(m_ref.shape, NEG, F32)
    l_ref[...] = jnp.zeros(l_ref.shape, F32)
    acc_ref[...] = jnp.zeros(acc_ref.shape, F32)


def _softmax_out(l_ref, acc_ref):
    l = l_ref[...]
    return acc_ref[...] / jnp.where(l > 0, l, 1.0)


def _mla_prompt_body(qt_ref, k_ref, kt_ref, wuv_ref, o_ref, m_ref, l_ref, acc_ref):
    i = pl.program_id(1)
    tq = kt_ref.shape[2]
    qt = qt_ref[0]
    _softmax_init(m_ref, l_ref, acc_ref)

    def tile(j):
        return k_ref[0, pl.ds(pl.multiple_of(j * tq, tq), tq), :], kt_ref[j, :MLA_KV_LORA, :]

    def step(j, carry):
        k, vt = tile(j)
        _softmax_step_t(_dot(k, qt), None, m_ref, l_ref, acc_ref, vt)
        return carry

    lax.fori_loop(0, i, step, 0)
    k, vt = tile(i)
    causal = (lax.broadcasted_iota(jnp.int32, (tq, tq), 0) <= lax.broadcasted_iota(jnp.int32, (tq, tq), 1))
    _softmax_step_t(_dot(k, qt), _lanes(jnp.where(causal, 1.0, 0.0), MLA_HEADS), m_ref, l_ref, acc_ref, vt)
    o_lat_t = acc_ref[...] / l_ref[...]
    for h in range(MLA_HEADS):
        o_ref[:, h * MLA_V:(h + 1) * MLA_V] = _dot(o_lat_t[:, h * tq:(h + 1) * tq].T, wuv_ref[h])


def _mla_prompt(qcat_t, rows_b, rows_t, wuv, l, b, s):
    tq = Q_TILE
    nq = s // tq
    rows = MLA_HEADS * tq
    return pl.pallas_call(
        _mla_prompt_body,
        grid=(b, nq),
        in_specs=[
            pl.BlockSpec((1, MLA_ROW_W, rows), lambda bi, i: (bi * nq + i, 0, 0)),
            pl.BlockSpec((1, s, MLA_ROW_W), lambda bi, i: (bi, 0, 0)),
            pl.BlockSpec((nq, MLA_ROW_W, tq), lambda bi, i: (bi, 0, 0)),
            pl.BlockSpec((None, MLA_HEADS, MLA_KV_LORA, MLA_V), lambda bi, i: (l, 0, 0, 0)),
        ],
        out_specs=pl.BlockSpec((tq, BRANCH_W), lambda bi, i: (bi * nq + i, 0)),
        out_shape=jax.ShapeDtypeStruct((b * s, BRANCH_W), F32),
        scratch_shapes=[pltpu.VMEM((1, rows), F32), pltpu.VMEM((1, rows), F32),
                        pltpu.VMEM((MLA_KV_LORA, rows), F32)],
        compiler_params=_cparams(("parallel", "arbitrary")),
        name="mla_prompt",
    )(qcat_t, rows_b.reshape(b, s, MLA_ROW_W), rows_t, wuv)


def _mla_sample_body(pt_ref, q_ref, *refs, pc, t_new):
    page_refs = refs[:pc]
    new_ref, wuv_ref, o_ref, m_ref, l_ref, acc_ref = refs[pc:]
    c = pl.program_id(1)

    @pl.when(c == 0)
    def _():
        _softmax_init(m_ref, l_ref, acc_ref)

    q = q_ref[0][:, :MLA_CACHE_W]
    state = (m_ref[...], l_ref[...], acc_ref[...])
    for g in range(0, pc, PAGE_GROUP):
        keys_t = jnp.concatenate([r[...].astype(BF16) for r in page_refs[g:g + PAGE_GROUP]], 1)
        state = _softmax_update(state, _dot(q, keys_t), None, MLA_HEADS, keys_t[:MLA_KV_LORA], vt=True)
    m_ref[...], l_ref[...], acc_ref[...] = state

    @pl.when(c == pl.num_programs(1) - 1)
    def _():
        k_new = new_ref[0][:, :MLA_CACHE_W]
        tq = lax.broadcasted_iota(jnp.int32, (t_new, t_new), 0)
        tk = lax.broadcasted_iota(jnp.int32, (t_new, t_new), 1)
        _softmax_step(_dot_nt(q, k_new), tk <= tq, MLA_HEADS, m_ref, l_ref, acc_ref, k_new[:, :MLA_KV_LORA])
        o_lat = _softmax_out(l_ref, acc_ref)
        for h in range(MLA_HEADS):
            o_ref[0, :, h * MLA_V:(h + 1) * MLA_V] = _dot(o_lat[h * t_new:(h + 1) * t_new], wuv_ref[h])


def _page_chunk(n_pages):
    return _pick_tile(n_pages, (64, 32, 16, 8, 4, 2, 1))


def _mla_sample(page_table, q, cache_t, rows_new, wuv, l):
    db, rows, _ = q.shape
    t_new = rows // MLA_HEADS
    n_pages = page_table.shape[1]
    page = cache_t.shape[3]
    pc = _page_chunk(n_pages)

    def page_spec(k):
        return pl.BlockSpec((None, None, MLA_CACHE_W, page),
                            lambda bi, c, pt: (l, pt[bi, c * pc + k], 0, 0))

    grid_spec = pltpu.PrefetchScalarGridSpec(
        num_scalar_prefetch=1,
        grid=(db, n_pages // pc),
        in_specs=[pl.BlockSpec((1, rows, MLA_ROW_W), lambda bi, c, pt: (bi, 0, 0))]
        + [page_spec(k) for k in range(pc)]
        + [pl.BlockSpec((1, t_new, MLA_ROW_W), lambda bi, c, pt: (bi, 0, 0)),
           pl.BlockSpec((None, MLA_HEADS, MLA_KV_LORA, MLA_V), lambda bi, c, pt: (l, 0, 0, 0))],
        out_specs=pl.BlockSpec((1, t_new, BRANCH_W), lambda bi, c, pt: (bi, 0, 0)),
        scratch_shapes=[pltpu.VMEM((rows, 1), F32), pltpu.VMEM((rows, 1), F32),
                        pltpu.VMEM((rows, MLA_KV_LORA), F32)],
    )
    return pl.pallas_call(
        functools.partial(_mla_sample_body, pc=pc, t_new=t_new),
        grid_spec=grid_spec,
        out_shape=jax.ShapeDtypeStruct((db, t_new, BRANCH_W), F32),
        compiler_params=_cparams(("parallel", "arbitrary")),
        name="mla_sample",
    )(page_table, q, *([cache_t] * pc), rows_new, wuv)


def _nsa_proj_body(zq_ref, zkv_ref, zg_ref, cq_ref, sq_ref, ckv_ref, skv_ref, q_ref, qt_ref, kv_ref, g_ref):
    tm = zq_ref.shape[0]
    q = _rope_apply(zq_ref[...], cq_ref[...], sq_ref[...]) * NSA_SCALE
    q_t = q.T
    for h in range(NSA_HEADS):
        q_ref[0, h] = q[:, h * NSA_HD:(h + 1) * NSA_HD].astype(BF16)
        qt_ref[0, :, h * tm:(h + 1) * tm] = q_t[h * NSA_HD:(h + 1) * NSA_HD].astype(BF16)
    kv_ref[...] = _rope_apply(zkv_ref[...], ckv_ref[...], skv_ref[...])
    g_ref[...] = jax.nn.sigmoid(zg_ref[...])


def _nsa_proj(zq, zkv, zg, tabs):
    n = zq.shape[0]
    tm = Q_TILE
    kvw = zkv.shape[1]
    row = lambda w: pl.BlockSpec((tm, w), lambda i: (i, 0))
    return pl.pallas_call(
        _nsa_proj_body,
        grid=(n // tm,),
        in_specs=[row(BRANCH_W), row(kvw), row(LANE), row(BRANCH_W), row(BRANCH_W), row(kvw), row(kvw)],
        out_specs=[pl.BlockSpec((1, NSA_HEADS, tm, NSA_HD), lambda i: (i, 0, 0, 0)),
                   pl.BlockSpec((1, NSA_HD, NSA_HEADS * tm), lambda i: (i, 0, 0)), row(kvw), row(LANE)],
        out_shape=[
            jax.ShapeDtypeStruct((n // tm, NSA_HEADS, tm, NSA_HD), BF16),
            jax.ShapeDtypeStruct((n // tm, NSA_HD, NSA_HEADS * tm), BF16),
            jax.ShapeDtypeStruct((n, kvw), F32),
            jax.ShapeDtypeStruct((n, LANE), F32),
        ],
        compiler_params=_cparams(("parallel",)),
        name="nsa_proj",
    )(zq, zkv, zg, *tabs)


def _bmean_prompt_body(kv_ref, o_ref):
    nb = o_ref.shape[1]
    x = kv_ref[0][:nb * NSA_BLOCK]
    o_ref[0] = jnp.sum(x.reshape(nb, NSA_BLOCK, LANE), 1) * (1.0 / NSA_BLOCK)


def _bmean_prompt(kv, b, s):
    nb = s // NSA_BLOCK
    return pl.pallas_call(
        _bmean_prompt_body,
        grid=(b,),
        in_specs=[pl.BlockSpec((1, s, LANE), lambda bi: (bi, 0, 0))],
        out_specs=pl.BlockSpec((1, nb, LANE), lambda bi: (bi, 0, 0)),
        out_shape=jax.ShapeDtypeStruct((b, nb, LANE), F32),
        compiler_params=_cparams(("parallel",)),
        name="bmean_prompt",
    )(kv.reshape(b, s, kv.shape[-1]))


def _bmean_sample_body(pt_ref, *refs, pc):
    a_ref, o_ref = refs[pc], refs[pc + 1]
    x = jnp.concatenate([r[...].astype(BF16) for r in refs[:pc]], 1)
    o_ref[0, 0] = jnp.dot(x, a_ref[...], preferred_element_type=F32) * (1.0 / NSA_BLOCK)


def _bmean_sample(page_table, cache_t, l):
    db, n_pages = page_table.shape
    page = cache_t.shape[3]
    pc = _page_chunk(n_pages)
    per = page // NSA_BLOCK
    nch = n_pages // pc
    ind = (jnp.arange(pc * page)[:, None] // NSA_BLOCK == jnp.arange(pc * per)[None, :]).astype(BF16)

    def page_spec(k):
        return pl.BlockSpec((None, None, 2 * NSA_HD, page), lambda bi, c, pt: (l, pt[bi, c * pc + k], 0, 0))

    grid_spec = pltpu.PrefetchScalarGridSpec(
        num_scalar_prefetch=1,
        grid=(db, nch),
        in_specs=[page_spec(k) for k in range(pc)]
        + [pl.BlockSpec((pc * page, pc * per), lambda bi, c, pt: (0, 0))],
        out_specs=pl.BlockSpec((1, 1, 2 * NSA_HD, pc * per), lambda bi, c, pt: (bi, c, 0, 0)),
    )
    out = pl.pallas_call(
        functools.partial(_bmean_sample_body, pc=pc),
        grid_spec=grid_spec,
        out_shape=jax.ShapeDtypeStruct((db, nch, 2 * NSA_HD, pc * per), F32),
        compiler_params=_cparams(("parallel", "arbitrary")),
        name="bmean_sample",
    )(page_table, *([cache_t] * pc), ind)
    return out.transpose(0, 2, 1, 3).reshape(db, 2 * NSA_HD, n_pages * per)


def _cmp_proj_t_body(x_ref, phi_ref, c_ref, s_ref, o_ref):
    half = NSA_HD // 2
    for bi in range(x_ref.shape[0]):
        y = _dot(phi_ref[...], x_ref[bi])
        yk = y[:NSA_HD]
        swapped = jnp.concatenate([yk[half:], yk[:half]], 0)
        o_ref[bi] = jnp.concatenate([yk * c_ref[...] + swapped * s_ref[...], y[NSA_HD:]], 0)


def _cmp_proj_t(means_t, phi_t, cos_t, sin_t, l):
    b, _, nb = means_t.shape
    bt = _pick_tile(b, (8, 4, 2, 1))
    return pl.pallas_call(
        _cmp_proj_t_body,
        grid=(b // bt,),
        in_specs=[
            pl.BlockSpec((bt, LANE, nb), lambda i: (i, 0, 0)),
            pl.BlockSpec((None, LANE, LANE), lambda i: (l, 0, 0)),
            pl.BlockSpec((NSA_HD, nb), lambda i: (0, 0)),
            pl.BlockSpec((NSA_HD, nb), lambda i: (0, 0)),
        ],
        out_specs=pl.BlockSpec((bt, LANE, nb), lambda i: (i, 0, 0)),
        out_shape=jax.ShapeDtypeStruct((b, LANE, nb), F32),
        compiler_params=_cparams(("parallel",)),
        name="cmp_proj_t",
    )(means_t, phi_t, cos_t, sin_t)


def _cmp_proj_body(x_ref, phi_ref, c_ref, s_ref, o_ref):
    bt, nb, _ = x_ref.shape
    y = _dot(x_ref[...].reshape(bt * nb, LANE), phi_ref[...]).reshape(bt, nb, LANE)
    o_ref[...] = _rope_apply(y, c_ref[...][None], s_ref[...][None])


def _cmp_proj(means, phi, cos, sin, l):
    b, nb, _ = means.shape
    bt = _pick_tile(b, (8, 4, 2, 1))
    return pl.pallas_call(
        _cmp_proj_body,
        grid=(b // bt,),
        in_specs=[
            pl.BlockSpec((bt, nb, LANE), lambda i: (i, 0, 0)),
            pl.BlockSpec((None, LANE, LANE), lambda i: (l, 0, 0)),
            pl.BlockSpec((nb, LANE), lambda i: (0, 0)),
            pl.BlockSpec((nb, LANE), lambda i: (0, 0)),
        ],
        out_specs=pl.BlockSpec((bt, nb, LANE), lambda i: (i, 0, 0)),
        out_shape=jax.ShapeDtypeStruct((b, nb, LANE), F32),
        compiler_params=_cparams(("parallel",)),
        name="cmp_proj",
    )(means, phi, cos, sin)


def _topk_select(imp, cand, k, axis):
    nb = imp.shape[axis]
    idx = lax.broadcasted_iota(jnp.int32, imp.shape, axis).astype(F32)
    v = imp if cand is None else jnp.where(cand, imp, -1.0)
    sel = jnp.zeros(imp.shape, F32)
    for _ in range(k):
        m = jnp.max(v, axis, keepdims=True)
        first = jnp.min(jnp.where(v == m, idx, float(nb)), axis, keepdims=True)
        hit = idx == first
        sel = jnp.where(hit & (m >= 0.0), 1.0, sel)
        v = jnp.where(hit, -2.0, v)
    return sel


def _nsa_prompt_body(qt_ref, kbvb_ref, kv_ref, g_ref, o_ref, m_ref, l_ref, acc_ref):
    i = pl.program_id(1)
    heads = NSA_HEADS
    tq = qt_ref.shape[2] // heads
    nb = kbvb_ref.shape[1]
    qt = qt_ref[0]
    kbvb = kbvb_ref[0]

    blk = lax.broadcasted_iota(jnp.int32, (nb, tq), 0)
    qpos_b = i * tq + lax.broadcasted_iota(jnp.int32, (nb, tq), 1)
    vis = jnp.where(blk * NSA_BLOCK + (NSA_BLOCK - 1) <= qpos_b, 1.0, 0.0)
    o_cmp, p = _softmax_once_t(_dot(kbvb[:, :NSA_HD], qt), _lanes(vis, heads), kbvb[:, NSA_HD:].T)
    imp = p[:, :tq]
    for h in range(1, heads):
        imp = imp + p[:, h * tq:(h + 1) * tq]
    cur = qpos_b // NSA_BLOCK
    sel = jnp.where(blk == cur, 1.0, _topk_select(imp, blk < cur, min(NSA_TOPN - 1, nb), 0))

    koff = lax.broadcasted_iota(jnp.int32, (tq, tq), 0)
    qpos = i * tq + lax.broadcasted_iota(jnp.int32, (tq, tq), 1)
    per = tq // NSA_BLOCK
    e_key = lax.broadcasted_iota(jnp.int32, (tq, nb), 0) // NSA_BLOCK
    e_blk = lax.broadcasted_iota(jnp.int32, (tq, nb), 1)

    def sel_step(j, carry):
        kv = kv_ref[0, pl.ds(pl.multiple_of(j * tq, tq), tq), 2 * NSA_HD:4 * NSA_HD]
        chosen = _dot(jnp.where(e_blk == j * per + e_key, 1.0, 0.0), sel)
        keep = jnp.where((chosen > 0.5) & (j * tq + koff <= qpos), 1.0, 0.0)
        _softmax_step_t(_dot(kv[:, :NSA_HD], qt), _lanes(keep, heads), m_ref, l_ref, acc_ref, kv[:, NSA_HD:].T)
        return carry

    _softmax_init(m_ref, l_ref, acc_ref)
    lax.fori_loop(0, i + 1, sel_step, 0)
    o_sel = acc_ref[...] / l_ref[...]

    def win_step(j, carry):
        kv = kv_ref[0, pl.ds(pl.multiple_of(j * tq, tq), tq), 4 * NSA_HD:6 * NSA_HD]
        rel = qpos - (j * tq + koff)
        keep = jnp.where((rel >= 0) & (rel < NSA_WINDOW), 1.0, 0.0)
        _softmax_step_t(_dot(kv[:, :NSA_HD], qt), _lanes(keep, heads), m_ref, l_ref, acc_ref, kv[:, NSA_HD:].T)
        return carry

    _softmax_init(m_ref, l_ref, acc_ref)
    lax.fori_loop(jnp.maximum(i - NSA_WINDOW // tq, 0), i + 1, win_step, 0)
    o_win = acc_ref[...] / l_ref[...]

    g_t = g_ref[...].T
    outs = []
    for h in range(heads):
        cols = slice(h * tq, (h + 1) * tq)
        outs.append(g_t[h:h + 1] * o_cmp[:, cols] + g_t[heads + h:heads + h + 1] * o_sel[:, cols]
                    + g_t[2 * heads + h:2 * heads + h + 1] * o_win[:, cols])
    o_ref[...] = jnp.concatenate(outs, 0).T


def _nsa_prompt(q_t, kbvb, kv, gates, b, s):
    tq = Q_TILE
    nq = s // tq
    nb = kbvb.shape[1]
    kvw = kv.shape[-1]
    rows = NSA_HEADS * tq
    return pl.pallas_call(
        _nsa_prompt_body,
        grid=(b, nq),
        in_specs=[
            pl.BlockSpec((1, NSA_HD, rows), lambda bi, i: (bi * nq + i, 0, 0)),
            pl.BlockSpec((1, nb, LANE), lambda bi, i: (bi, 0, 0)),
            pl.BlockSpec((1, s, kvw), lambda bi, i: (bi, 0, 0)),
            pl.BlockSpec((tq, LANE), lambda bi, i: (bi * nq + i, 0)),
        ],
        out_specs=pl.BlockSpec((tq, BRANCH_W), lambda bi, i: (bi * nq + i, 0)),
        out_shape=jax.ShapeDtypeStruct((b * s, BRANCH_W), F32),
        scratch_shapes=[pltpu.VMEM((1, rows), F32), pltpu.VMEM((1, rows), F32), pltpu.VMEM((NSA_HD, rows), F32)],
        compiler_params=_cparams(("parallel", "arbitrary")),
        name="nsa_prompt",
    )(q_t, kbvb, kv.reshape(b, s, kvw), gates)


def _nsa_cmp_sample_body(q_ref, kt_ref, o_ref, sel_ref, *, t_new):
    bt, _, nb = kt_ref.shape
    imps = []
    for bi in range(bt):
        o, p = _softmax_once(_dot(q_ref[bi], kt_ref[bi, :NSA_HD, :]), None, NSA_HEADS, kt_ref[bi, NSA_HD:, :],
                             vt=True)
        o_ref[bi] = o
        imps.append(jnp.sum(p.reshape(NSA_HEADS, t_new, nb), 0))
    sel = _topk_select(jnp.concatenate(imps, 0), None, min(NSA_TOPN - 1, nb), 1)
    sel_ref[...] = sel.reshape(bt, t_new, nb)


def _nsa_cmp_sample(q, kbvb_t, t_new):
    db, rows, _ = q.shape
    nb = kbvb_t.shape[2]
    bt = _pick_tile(db, (8, 4, 2, 1))
    return pl.pallas_call(
        functools.partial(_nsa_cmp_sample_body, t_new=t_new),
        grid=(db // bt,),
        in_specs=[pl.BlockSpec((bt, rows, NSA_HD), lambda bi: (bi, 0, 0)),
                  pl.BlockSpec((bt, LANE, nb), lambda bi: (bi, 0, 0))],
        out_specs=[pl.BlockSpec((bt, rows, NSA_HD), lambda bi: (bi, 0, 0)),
                   pl.BlockSpec((bt, t_new, nb), lambda bi: (bi, 0, 0))],
        out_shape=[jax.ShapeDtypeStruct((db, rows, NSA_HD), F32), jax.ShapeDtypeStruct((db, t_new, nb), F32)],
        compiler_params=_cparams(("parallel",)),
        name="nsa_cmp_sample",
    )(q, kbvb_t)


def _nsa_sel_sample_body(pt_ref, q_ref, sel_ref, exp_ref, *refs, pc, t_new):
    page_refs = refs[:pc]
    new_ref, o_ref, m_ref, l_ref, acc_ref = refs[pc:]
    c = pl.program_id(1)

    @pl.when(c == 0)
    def _():
        _softmax_init(m_ref, l_ref, acc_ref)

    q = q_ref[0]
    page = page_refs[0].shape[1]
    chosen = _dot(sel_ref[0, 0], exp_ref[...])
    state = (m_ref[...], l_ref[...], acc_ref[...])
    for g in range(0, pc, PAGE_GROUP):
        kv_t = jnp.concatenate([r[...].astype(BF16) for r in page_refs[g:g + PAGE_GROUP]], 1)
        state = _softmax_update(state, _dot(q, kv_t[:NSA_HD]), chosen[:, g * page:(g + PAGE_GROUP) * page] > 0.5,
                                NSA_HEADS, kv_t[NSA_HD:], vt=True)
    m_ref[...], l_ref[...], acc_ref[...] = state

    @pl.when(c == pl.num_programs(1) - 1)
    def _():
        kv_new = new_ref[0]
        tq = lax.broadcasted_iota(jnp.int32, (t_new, t_new), 0)
        tk = lax.broadcasted_iota(jnp.int32, (t_new, t_new), 1)
        _softmax_step(_dot_nt(q, kv_new[:, :NSA_HD]), tk <= tq, NSA_HEADS, m_ref, l_ref, acc_ref,
                      kv_new[:, NSA_HD:])
        o_ref[0] = _softmax_out(l_ref, acc_ref)


def _nsa_sel_sample(page_table, q, sel, cache_t, kv_new, l, t_new):
    db, rows, _ = q.shape
    n_pages = page_table.shape[1]
    page = cache_t.shape[3]
    pc = _page_chunk(n_pages)
    per = page // NSA_BLOCK
    nch = n_pages // pc
    sel_c = sel.reshape(db, t_new, nch, pc * per).transpose(0, 2, 1, 3)
    expand = (jnp.arange(pc * per)[:, None] == jnp.arange(pc * page)[None, :] // NSA_BLOCK).astype(BF16)

    def page_spec(k):
        return pl.BlockSpec((None, None, 2 * NSA_HD, page), lambda bi, c, pt: (l, pt[bi, c * pc + k], 1, 0))

    grid_spec = pltpu.PrefetchScalarGridSpec(
        num_scalar_prefetch=1,
        grid=(db, nch),
        in_specs=[pl.BlockSpec((1, rows, NSA_HD), lambda bi, c, pt: (bi, 0, 0)),
                  pl.BlockSpec((1, 1, t_new, pc * per), lambda bi, c, pt: (bi, c, 0, 0)),
                  pl.BlockSpec((pc * per, pc * page), lambda bi, c, pt: (0, 0))]
        + [page_spec(k) for k in range(pc)]
        + [pl.BlockSpec((1, t_new, LANE), lambda bi, c, pt: (bi, 0, 0))],
        out_specs=pl.BlockSpec((1, rows, NSA_HD), lambda bi, c, pt: (bi, 0, 0)),
        scratch_shapes=[pltpu.VMEM((rows, 1), F32), pltpu.VMEM((rows, 1), F32), pltpu.VMEM((rows, NSA_HD), F32)],
    )
    return pl.pallas_call(
        functools.partial(_nsa_sel_sample_body, pc=pc, t_new=t_new),
        grid_spec=grid_spec,
        out_shape=jax.ShapeDtypeStruct((db, rows, NSA_HD), F32),
        compiler_params=_cparams(("parallel", "arbitrary")),
        name="nsa_sel_sample",
    )(page_table, q, sel_c, expand, *([cache_t] * pc), kv_new)


def _nsa_win_sample_body(q_ref, buf_ref, new_ref, ocmp_ref, osel_ref, g_ref, o_ref, *, t_new):
    heads = NSA_HEADS
    lw = buf_ref.shape[2]
    q = q_ref[0]
    buf_t = buf_ref[0]
    kv_new = new_ref[0]
    rel_b = (lax.broadcasted_iota(jnp.int32, (t_new, lw), 0) + lw
             - lax.broadcasted_iota(jnp.int32, (t_new, lw), 1))
    mask_b = (rel_b >= 0) & (rel_b < NSA_WINDOW)
    mask_n = (lax.broadcasted_iota(jnp.int32, (t_new, t_new), 1)
              <= lax.broadcasted_iota(jnp.int32, (t_new, t_new), 0))
    s_b = _masked(_dot(q, buf_t[:NSA_HD]), mask_b, heads, NEG)
    s_n = _masked(_dot_nt(q, kv_new[:, :NSA_HD]), mask_n, heads, NEG)
    m = jnp.maximum(jnp.max(s_b, -1, keepdims=True), jnp.max(s_n, -1, keepdims=True))
    e_b = _masked(jnp.exp(s_b - m), mask_b, heads, 0.0)
    e_n = _masked(jnp.exp(s_n - m), mask_n, heads, 0.0)
    den = jnp.sum(e_b, -1, keepdims=True) + jnp.sum(e_n, -1, keepdims=True)
    o_win = (_dot_nt(e_b, buf_t[NSA_HD:]) + _dot(e_n, kv_new[:, NSA_HD:])) / jnp.where(den > 0, den, 1.0)
    o_cmp = ocmp_ref[0]
    o_sel = osel_ref[0]
    g = g_ref[0]
    for h in range(heads):
        rows = slice(h * t_new, (h + 1) * t_new)
        o_ref[0, :, h * NSA_HD:(h + 1) * NSA_HD] = (
            g[:, h:h + 1] * o_cmp[rows] + g[:, heads + h:heads + h + 1] * o_sel[rows]
            + g[:, 2 * heads + h:2 * heads + h + 1] * o_win[rows])


def _nsa_win_sample(q, buf_t, kv_new, o_cmp, o_sel, gates, t_new):
    db, rows, _ = q.shape
    lw = buf_t.shape[2]
    blk = lambda r, w: pl.BlockSpec((1, r, w), lambda bi: (bi, 0, 0))
    return pl.pallas_call(
        functools.partial(_nsa_win_sample_body, t_new=t_new),
        grid=(db,),
        in_specs=[blk(rows, NSA_HD), blk(LANE, lw), blk(t_new, LANE), blk(rows, NSA_HD), blk(rows, NSA_HD),
                  blk(t_new, LANE)],
        out_specs=blk(t_new, BRANCH_W),
        out_shape=jax.ShapeDtypeStruct((db, t_new, BRANCH_W), F32),
        compiler_params=_cparams(("parallel",)),
        name="nsa_win_sample",
    )(q, buf_t, kv_new, o_cmp, o_sel, gates)


def _rwkv_pre_body(zr_ref, zp_ref, mu_ref, w0_ref, a0_ref, wa_ref, g2_ref, kk_ref, ka_ref, ones_ref,
                   r_ref, k_ref, v_ref, lw_ref, kkn_ref, b_ref, g_ref):
    c = BRANCH_W
    zr = zr_ref[...]
    zs = zr + (zp_ref[...] - zr) * mu_ref[...]
    r, k, v = zs[:, :c], zs[:, c:2 * c], zs[:, 2 * c:3 * c]
    lora = zs[:, 3 * c:3 * c + LANE]
    lane = lax.broadcasted_iota(jnp.int32, lora.shape, 1)
    wa = _dot(jnp.where(lane < RWKV_W_LORA, jnp.tanh(lora), lora), wa_ref[...])
    x = -(w0_ref[...] + wa[:, :c])
    softplus = jnp.maximum(x, 0.0) + jnp.log(1.0 + jnp.exp(-jnp.abs(x)))
    a = jax.nn.sigmoid(a0_ref[...] + wa[:, c:])
    kk = k * kk_ref[...]
    norm = jnp.sqrt(_dot_split(kk * kk, ones_ref[...]))
    kk = kk / jnp.maximum(norm, 1e-12)
    r_ref[...] = r
    k_ref[...] = k * (1.0 + (a - 1.0) * ka_ref[...])
    v_ref[...] = v
    lw_ref[...] = -jnp.exp(-softplus - 0.5)
    kkn_ref[...] = kk
    b_ref[...] = kk * a
    g_ref[...] = _dot(jax.nn.sigmoid(zs[:, 3 * c + LANE:]), g2_ref[...])


def _rwkv_pre(zr, zprev, mu, w0, a0, wa, g2, k_k, k_a, ones_bd, l):
    n = zr.shape[0]
    tm = _pick_tile(n, (256, 128))
    c = BRANCH_W
    row = lambda w: pl.BlockSpec((tm, w), lambda i: (i, 0))
    vec = lambda w: _vec_spec(w, l, 1)
    return pl.pallas_call(
        _rwkv_pre_body,
        grid=(n // tm,),
        in_specs=[row(RWKV_IN), row(RWKV_IN), vec(RWKV_IN), vec(c), vec(c),
                  pl.BlockSpec((None, LANE, 2 * c), lambda i: (l, 0, 0)),
                  pl.BlockSpec((None, RWKV_G_LORA, c), lambda i: (l, 0, 0)),
                  vec(c), vec(c), pl.BlockSpec((c, c), lambda i: (0, 0))],
        out_specs=[row(c)] * 7,
        out_shape=[jax.ShapeDtypeStruct((n, c), F32)] * 7,
        compiler_params=_cparams(("parallel",)),
        name="rwkv_pre",
    )(zr, zprev, _vec3(mu), _vec3(w0), _vec3(a0), wa, g2, _vec3(k_k), _vec3(k_a), ones_bd)


def _rwkv_scan_body(r_ref, k_ref, v_ref, lw_ref, kk_ref, b_ref, s0_ref, y_ref, sf_ref, st_ref, *, c_len, nh):
    ci = pl.program_id(1)

    @pl.when(ci == 0)
    def _():
        st_ref[...] = s0_ref[...]

    for si in range(st_ref.shape[0]):
        _rwkv_scan_seq(si, r_ref, k_ref, v_ref, lw_ref, kk_ref, b_ref, y_ref, st_ref, c_len, nh)

    @pl.when(ci == pl.num_programs(1) - 1)
    def _():
        sf_ref[...] = st_ref[...]


def _rwkv_scan_seq(si, r_ref, k_ref, v_ref, lw_ref, kk_ref, b_ref, y_ref, st_ref, c_len, nh):
    ch = nh * RWKV_HS
    rows = nh * c_len
    lw = lw_ref[si]
    tri = (lax.broadcasted_iota(jnp.int32, (c_len, c_len), 0)
           >= lax.broadcasted_iota(jnp.int32, (c_len, c_len), 1)).astype(BF16)
    lp = None
    rem = lw
    for _ in range(3):
        piece = rem.astype(BF16)
        d = jnp.dot(tri, piece, preferred_element_type=F32)
        lp = d if lp is None else lp + d
        rem = rem - piece.astype(F32)
    p = jnp.exp(lp)
    p_inv = jnp.exp(-lp)
    hm = (lax.broadcasted_iota(jnp.int32, (rows, ch), 0) // c_len
          == lax.broadcasted_iota(jnp.int32, (rows, ch), 1) // RWKV_HS)

    def stack(a):
        return jnp.where(hm, jnp.concatenate([a] * nh, 0), 0.0).astype(BF16)

    x_kk = stack(kk_ref[si] * jnp.exp(lp - lw))
    x_r = stack(r_ref[si] * p)
    y_k = stack(k_ref[si] * p_inv)
    y_b = stack(b_ref[si] * p_inv)
    v = v_ref[si]
    vs = jnp.concatenate([v[:, h * RWKV_HS:(h + 1) * RWKV_HS] for h in range(nh)], 0)

    ti = lax.broadcasted_iota(jnp.int32, (rows, rows), 0) % c_len
    tj = lax.broadcasted_iota(jnp.int32, (rows, rows), 1) % c_len
    strict, incl = ti > tj, ti >= tj
    n_mat = jnp.where(strict, -_dot_nt(x_kk, y_b), 0.0)
    a_kk = jnp.where(strict, _dot_nt(x_kk, y_k), 0.0)
    a_rk = jnp.where(incl, _dot_nt(x_r, y_k), 0.0)
    a_rb = jnp.where(incl, _dot_nt(x_r, y_b), 0.0)
    eye = jnp.where(lax.broadcasted_iota(jnp.int32, (rows, rows), 0)
                    == lax.broadcasted_iota(jnp.int32, (rows, rows), 1), 1.0, 0.0)
    t_inv = eye + n_mat
    pw = n_mat
    span = 2
    while span < c_len:
        pw = _dot(pw, pw)
        t_inv = t_inv + _dot(t_inv, pw)
        span *= 2

    st = st_ref[si]
    u = _dot(t_inv, _dot_nt(x_kk, st) + _dot(a_kk, vs))
    y = _dot_nt(x_r, st) + _dot(a_rk, vs) - _dot(a_rb, u)
    st_ref[si] = p[c_len - 1:c_len, :] * (st + _dot(vs.T, y_k) - _dot(u.T, y_b))
    for h in range(nh):
        y_ref[si, :, h * RWKV_HS:(h + 1) * RWKV_HS] = y[h * c_len:(h + 1) * c_len]


def _rwkv_scan(r, k, v, lw, kk, b, s0, c_len):
    nseq, t, ch = r.shape
    nh = ch // RWKV_HS
    assert nh * c_len == SCAN_ROWS
    ns = _pick_tile(nseq, (2, 1))
    blk = pl.BlockSpec((ns, c_len, ch), lambda si, ci: (si, ci, 0))
    st = pl.BlockSpec((ns, RWKV_HS, ch), lambda si, ci: (si, 0, 0))
    return pl.pallas_call(
        functools.partial(_rwkv_scan_body, c_len=c_len, nh=nh),
        grid=(nseq // ns, t // c_len),
        in_specs=[blk] * 6 + [st],
        out_specs=[blk, st],
        out_shape=[jax.ShapeDtypeStruct((nseq, t, ch), F32), jax.ShapeDtypeStruct((nseq, RWKV_HS, ch), F32)],
        scratch_shapes=[pltpu.VMEM((ns, RWKV_HS, ch), F32)],
        compiler_params=_cparams(("parallel", "arbitrary")),
        name="rwkv_scan",
    )(r, k, v, lw, kk, b, s0)


def _rwkv_post_body(y_ref, r_ref, k_ref, v_ref, g_ref, rk_ref, lnw_ref, lnb_ref, ones_ref, o_ref):
    y = y_ref[...]
    ones = ones_ref[...]
    mu = _dot_split(y, ones) * (1.0 / RWKV_HS)
    dev = y - mu
    var = _dot_split(dev * dev, ones) * (1.0 / RWKV_HS)
    yn = dev * lax.rsqrt(var + RWKV_GN_EPS) * lnw_ref[...] + lnb_ref[...]
    bonus = _dot_split(r_ref[...] * k_ref[...] * rk_ref[...], ones) * v_ref[...]
    o_ref[...] = (yn + bonus) * g_ref[...]


def _rwkv_post(y, r, k, v, g, r_k, ln_w, ln_b, ones_bd, l):
    n, c = y.shape
    tm = _pick_tile(n, (256, 128))
    row = pl.BlockSpec((tm, c), lambda i: (i, 0))
    vec = _vec_spec(c, l, 1)
    return pl.pallas_call(
        _rwkv_post_body,
        grid=(n // tm,),
        in_specs=[row] * 5 + [vec] * 3 + [pl.BlockSpec((c, c), lambda i: (0, 0))],
        out_specs=row,
        out_shape=jax.ShapeDtypeStruct((n, c), F32),
        compiler_params=_cparams(("parallel",)),
        name="rwkv_post",
    )(y, r, k, v, g, _vec3(r_k), _vec3(ln_w), _vec3(ln_b), ones_bd)


def _rope_tables(pos):
    inv = ROPE_THETA ** (-jnp.arange(0, NSA_HD, 2, dtype=F32) / NSA_HD)
    ang = pos.astype(F32)[:, None] * inv[None, :]
    c, s = jnp.cos(ang), jnp.sin(ang)
    return jnp.concatenate([c, c], -1), jnp.concatenate([-s, s], -1)


def _table(cos, sin, groups):
    n = cos.shape[0]
    cs = [cos if rot else jnp.ones((n, w), F32) for w, rot in groups]
    ss = [sin if rot else jnp.zeros((n, w), F32) for w, rot in groups]
    return jnp.concatenate(cs, -1), jnp.concatenate(ss, -1)


def _heads_first(x, tile, group, t_new):
    nt, h, _, d = x.shape
    x = x.reshape(nt, h, tile // t_new, t_new, d).transpose(0, 2, 1, 3, 4)
    return x.reshape(group, h * t_new, d)


def kernel(x_prompt, x_sample, cache_mla, cache_nsa, state_nsa_win, state_pool, state_rwkv, state_rwkv_shift,
           page_table, norm_ffn1, ffn1_w_gate, ffn1_w_up, ffn1_w_down, norm_mix, w_in, w_branch, w_out, pool_w,
           pool_scale, mla_q_norm, mla_w_uq, mla_kv_norm, mla_w_uk, mla_w_uv, nsa_phi_k, nsa_phi_v, rwkv_mu,
           rwkv_w0, rwkv_w2, rwkv_a0, rwkv_a2, rwkv_g2, rwkv_k_k, rwkv_k_a, rwkv_r_k, rwkv_ln_w, rwkv_ln_b,
           norm_ffn2, ffn2_w_gate, ffn2_w_up, ffn2_w_down, norm_final):
    b, s, d = x_prompt.shape
    db, t_new = x_sample.shape[:2]
    depth = norm_ffn1.shape[0]
    n_p, n_s = b * s, db * t_new
    n_pages, page = page_table.shape[1], cache_mla.shape[2]
    past = n_pages * page
    win_len = state_nsa_win.shape[2]
    n_phys = cache_mla.shape[1]
    assert d == D_MODEL and s % Q_TILE == 0 and n_s % Q_TILE == 0 and Q_TILE % t_new == 0
    assert s >= win_len and page % NSA_BLOCK == 0 and t_new < NSA_BLOCK and t_new <= POOL_HALO
    samp_c = t_new
    samp_grp = SCAN_ROWS // (samp_c * RWKV_HEADS)
    assert db % samp_grp == 0 and s % 32 == 0

    bf = lambda w: w.astype(BF16)
    tf = _pick_tile(ffn1_w_gate.shape[2], (512, 256, 128))
    f1 = (_col_tiles(bf(ffn1_w_gate), tf), _col_tiles(bf(ffn1_w_up), tf), bf(ffn1_w_down))
    f2 = (_col_tiles(bf(ffn2_w_gate), tf), _col_tiles(bf(ffn2_w_up), tf), bf(ffn2_w_down))
    o_mla = BRANCH_W
    o_nsa = o_mla + MLA_Q_LORA + MLA_CACHE_W
    o_rw = o_nsa + BRANCH_W + 6 * NSA_HD + 3 * NSA_HEADS
    o_gate = o_rw + RWKV_IN
    zpad = lambda w: jnp.zeros((depth, d, w), w_in.dtype)
    w_in_p = _col_tiles(bf(jnp.concatenate([
        w_in[:, :, o_gate:], w_in[:, :, :o_mla], w_in[:, :, o_mla:o_mla + MLA_Q_LORA],
        w_in[:, :, o_mla + MLA_Q_LORA:o_nsa], zpad(MLA_ROW_W - MLA_CACHE_W),
        w_in[:, :, o_nsa:o_nsa + BRANCH_W], w_in[:, :, o_nsa + BRANCH_W:o_nsa + BRANCH_W + 6 * NSA_HD],
        w_in[:, :, o_nsa + BRANCH_W + 6 * NSA_HD:o_rw], zpad(LANE - 3 * NSA_HEADS),
        w_in[:, :, o_rw:o_gate], zpad(Z_W - Z_END)], -1)), Z_TILE)
    wb = bf(w_branch)
    wo = bf(w_out)
    pw = bf(pool_w)
    uq = mla_w_uq.reshape(depth, MLA_Q_LORA, MLA_HEADS, MLA_NOPE + MLA_ROPE)
    wuq = bf(jnp.concatenate([uq[..., :MLA_NOPE].reshape(depth, MLA_Q_LORA, -1),
                              uq[..., MLA_NOPE:].reshape(depth, MLA_Q_LORA, -1)], -1))
    wuk = bf(mla_w_uk.transpose(0, 2, 3, 1))
    wuv = bf(mla_w_uv.transpose(0, 2, 1, 3))
    zz = jnp.zeros((depth, NSA_HD, NSA_HD), F32)
    phi = bf(jnp.concatenate([jnp.concatenate([nsa_phi_k, zz], -1), jnp.concatenate([zz, nsa_phi_v], -1)], 1))
    phi_t = phi.transpose(0, 2, 1)
    zw =jnp.zeros((depth, RWKV_W_LORA, BRANCH_W), F32)
    wa = bf(jnp.concatenate([jnp.concatenate([rwkv_w2, zw], -1), jnp.concatenate([zw, rwkv_a2], -1)], 1))
    g2 = bf(rwkv_g2)
    r_k = rwkv_r_k.reshape(depth, BRANCH_W)
    ones_bd = (jnp.arange(BRANCH_W)[:, None] // RWKV_HS == jnp.arange(BRANCH_W)[None, :] // RWKV_HS).astype(BF16)

    pos = jnp.concatenate([jnp.tile(jnp.arange(s), b), jnp.tile(past + jnp.arange(t_new), db)])
    cos, sin = _rope_tables(pos)
    tab_mq = _table(cos, sin, [(MLA_HEADS * MLA_NOPE, False)] + [(MLA_ROPE, True)] * MLA_HEADS)
    tab_mkv = _table(cos, sin, [(MLA_KV_LORA, False), (MLA_ROPE, True), (MLA_ROW_W - MLA_CACHE_W, False)])
    tab_nq = _table(cos, sin, [(NSA_HD, True)] * NSA_HEADS)
    tab_nkv = _table(cos, sin, [(2 * NSA_HD, False), (NSA_HD, True), (NSA_HD, False), (NSA_HD, True),
                                (NSA_HD, False)])
    nb_p = s // NSA_BLOCK
    nb_s = past // NSA_BLOCK
    tab_bp = _table(*_rope_tables(jnp.arange(nb_p) * NSA_BLOCK + NSA_BLOCK - 1), [(NSA_HD, True), (NSA_HD, False)])
    tab_bs = [t.T for t in _rope_tables(jnp.arange(nb_s) * NSA_BLOCK + NSA_BLOCK - 1)]

    cache_mla_t = cache_mla.transpose(0, 1, 3, 2)
    cache_nsa_t = cache_nsa.transpose(0, 1, 3, 4, 2).reshape(depth, n_phys, 4 * NSA_HD, page)
    x = jnp.concatenate([x_prompt.reshape(n_p, d), x_sample.reshape(n_s, d)], 0)
    ns_tiles = n_s // Q_TILE
    new_p = [[] for _ in range(6)]
    new_s = [[] for _ in range(6)]
    for l in range(depth):
        x = _ffn(x, norm_ffn1[l], *f1, norm_final, l, False)
        z = _inproj(x, norm_mix[l], w_in_p, l)

        zp = z[:, Z_POOL:Z_POOL + BRANCH_W]
        zp_p = zp[:n_p].reshape(b, s, BRANCH_W)
        zp_s = zp[n_p:].reshape(db, t_new, BRANCH_W)
        pre_p = jnp.zeros((b, POOL_HALO, BRANCH_W), F32)
        pre_s = jnp.concatenate([jnp.zeros((db, POOL_HALO - POOL_BUF, BRANCH_W), F32), state_pool[l]], 1)
        o_pool_p = _pool(jnp.concatenate([pre_p, zp_p], 1), pw, pool_scale, l, 0, 1)
        o_pool_s = _pool(jnp.concatenate([pre_s, zp_s], 1), pw, pool_scale, l, past, _pick_tile(db, (32, 16, 8, 1)))
        o_pool = jnp.concatenate([o_pool_p.reshape(n_p, BRANCH_W), o_pool_s.reshape(n_s, BRANCH_W)], 0)
        new_p[3].append(zp_p[:, -POOL_BUF:])
        new_s[3].append(jnp.concatenate([state_pool[l], zp_s], 1)[:, -POOL_BUF:])

        qcat, qcat_t, rows, rows_b, rows_t = _mla_proj(
            z[:, Z_MQ:Z_MQ + MLA_Q_LORA], z[:, Z_MKV:Z_MKV + MLA_ROW_W], mla_q_norm, mla_kv_norm, wuq, wuk,
            tab_mq + tab_mkv, l)
        o_mla_p = _mla_prompt(qcat_t, rows_b[:n_p], rows_t, wuv, l, b, s)
        q_s = _heads_first(qcat[n_p // Q_TILE:], Q_TILE, db, t_new)
        o_mla_s = _mla_sample(page_table, q_s, cache_mla_t, rows_b[n_p:].reshape(db, t_new, MLA_ROW_W), wuv, l)
        o_mla_all = jnp.concatenate([o_mla_p, o_mla_s.reshape(n_s, BRANCH_W)], 0)
        new_p[0].append(rows[:n_p, :MLA_CACHE_W].reshape(b, s, MLA_CACHE_W))
        new_s[0].append(rows[n_p:, :MLA_CACHE_W].reshape(db, t_new, MLA_CACHE_W))

        nq, nq_t, nkv, ng = _nsa_proj(z[:, Z_NQ:Z_NQ + BRANCH_W], z[:, Z_NKV:Z_NKV + 6 * NSA_HD],
                                      z[:, Z_NG:Z_NG + LANE], tab_nq + tab_nkv)
        kbvb_p = _cmp_proj(_bmean_prompt(nkv[:n_p], b, s), phi, *tab_bp, l)
        o_nsa_p = _nsa_prompt(nq_t, kbvb_p, nkv[:n_p], ng[:n_p], b, s)
        nq_s = _heads_first(nq[n_p // Q_TILE:], Q_TILE, db, t_new)
        nkv_s = nkv[n_p:].reshape(db, t_new, 6 * NSA_HD)
        kbvb_s = _cmp_proj_t(_bmean_sample(page_table, cache_nsa_t, l), phi_t, *tab_bs, l)
        o_cmp_s, sel_s = _nsa_cmp_sample(nq_s, kbvb_s, t_new)
        o_sel_s = _nsa_sel_sample(page_table, nq_s, sel_s, cache_nsa_t, nkv_s[:, :, 2 * NSA_HD:4 * NSA_HD], l, t_new)
        win_buf_t = state_nsa_win[l].transpose(0, 2, 3, 1).reshape(db, 2 * NSA_HD, win_len)
        o_nsa_s = _nsa_win_sample(nq_s, win_buf_t, nkv_s[:, :, 4 * NSA_HD:], o_cmp_s, o_sel_s,
                                  ng[n_p:].reshape(db, t_new, LANE), t_new)
        o_nsa_all = jnp.concatenate([o_nsa_p, o_nsa_s.reshape(n_s, BRANCH_W)], 0)
        nkv_p = nkv[:n_p].reshape(b, s, 6, NSA_HD)
        new_p[1].append(nkv_p[:, :, :4])
        new_s[1].append(nkv_s[:, :, :4 * NSA_HD].reshape(db, t_new, 4, NSA_HD))
        new_p[2].append(nkv_p[:, s - win_len:, 4:])
        new_s[2].append(jnp.concatenate([state_nsa_win[l], nkv_s[:, :, 4 * NSA_HD:].reshape(db, t_new, 2, NSA_HD)],
                                        1)[:, -win_len:])

        zr = z[:, Z_RW:Z_RW + RWKV_IN]
        zr_p = zr[:n_p].reshape(b, s, RWKV_IN)
        zr_s = zr[n_p:].reshape(db, t_new, RWKV_IN)
        zprev = jnp.concatenate([
            jnp.concatenate([jnp.zeros((b, 1, RWKV_IN), F32), zr_p[:, :-1]], 1).reshape(n_p, RWKV_IN),
            jnp.concatenate([state_rwkv_shift[l][:, None], zr_s[:, :-1]], 1).reshape(n_s, RWKV_IN)], 0)
        pre = _rwkv_pre(zr, zprev, rwkv_mu, rwkv_w0, rwkv_a0, wa, g2, rwkv_k_k, rwkv_k_a, ones_bd, l)
        r_, k_, v_, lw_, kk_, b_, g_ = pre
        seq_p = lambda a: a[:n_p].reshape(b, s, BRANCH_W)
        seq_s = lambda a: (a[n_p:].reshape(db // samp_grp, samp_grp, t_new, BRANCH_W).transpose(0, 2, 1, 3)
                           .reshape(db // samp_grp, t_new, samp_grp * BRANCH_W))
        scan_in = (r_, k_, v_, lw_, kk_, b_)
        y_p, sf_p = _rwkv_scan(*[seq_p(a) for a in scan_in],
                               jnp.zeros((b, RWKV_HS, BRANCH_W), F32), SCAN_ROWS // RWKV_HEADS)
        s0_s = (state_rwkv[l].reshape(db // samp_grp, samp_grp, RWKV_HEADS, RWKV_HS, RWKV_HS)
                .transpose(0, 3, 1, 2, 4).reshape(db // samp_grp, RWKV_HS, samp_grp * BRANCH_W))
        y_s, sf_s = _rwkv_scan(*[seq_s(a) for a in scan_in], s0_s, samp_c)
        y_s = (y_s.reshape(db // samp_grp, t_new, samp_grp, BRANCH_W).transpose(0, 2, 1, 3)
               .reshape(n_s, BRANCH_W))
        y_all = jnp.concatenate([y_p.reshape(n_p, BRANCH_W), y_s], 0)
        o_rwkv = _rwkv_post(y_all, r_, k_, v_, g_, r_k, rwkv_ln_w, rwkv_ln_b, ones_bd, l)
        new_p[4].append(sf_p.reshape(b, RWKV_HS, RWKV_HEADS, RWKV_HS).transpose(0, 2, 1, 3))
        new_s[4].append(sf_s.reshape(db // samp_grp, RWKV_HS, samp_grp, RWKV_HEADS, RWKV_HS)
                        .transpose(0, 2, 3, 1, 4).reshape(db, RWKV_HEADS, RWKV_HS, RWKV_HS))
        new_p[5].append(zr_p[:, -1])
        new_s[5].append(zr_s[:, -1])

        branches = jnp.concatenate([o_pool, o_mla_all, o_nsa_all, o_rwkv], -1)
        x = _merge(x, branches, z, wb, wo, l)
        x = _ffn(x, norm_ffn2[l], *f2, norm_final, l, l == depth - 1)

    y_prompt = x[:n_p].reshape(b, s, d)
    y_sample = x[n_p:].reshape(db, t_new, d)
    outs_p = [jnp.stack(a, 0) for a in new_p]
    outs_s = [jnp.stack(a, 0) for a in new_s]
    res = [y_prompt, y_sample]
    for a, c in zip(outs_p, outs_s):
        res += [a, c]
    return tuple(res)
```

```python
import functools

import jax
import jax.numpy as jnp
from jax import lax
from jax.experimental import pallas as pl
from jax.experimental.pallas import tpu as pltpu

F32 = jnp.float32
BF16 = jnp.bfloat16

D_MODEL = 2048
N_BRANCH = 4
BRANCH_W = D_MODEL // N_BRANCH
ROPE_THETA = 10000.0
RMS_EPS = 1e-6
NEG = -1e30
MASK_BIAS = -2e30
POOL_WINDOWS = (2, 4, 8, 16)
POOL_GW = BRANCH_W // len(POOL_WINDOWS)
POOL_BUF = max(POOL_WINDOWS) - 1
POOL_HALO = 16
MLA_HEADS = 4
MLA_NOPE = 128
MLA_ROPE = 64
MLA_V = BRANCH_W // MLA_HEADS
MLA_Q_LORA = D_MODEL // 4
MLA_KV_LORA = D_MODEL // 8
MLA_CACHE_W = MLA_KV_LORA + MLA_ROPE
MLA_ROW_W = 384
MLA_SCALE = (MLA_NOPE + MLA_ROPE) ** -0.5
NSA_HEADS = 8
NSA_HD = BRANCH_W // NSA_HEADS
NSA_BLOCK = 64
NSA_TOPN = 16
NSA_WINDOW = 512
NSA_SCALE = NSA_HD ** -0.5
RWKV_HS = 64
RWKV_HEADS = BRANCH_W // RWKV_HS
RWKV_W_LORA = 64
RWKV_A_LORA = 64
RWKV_G_LORA = 128
RWKV_GN_EPS = 64e-5
RWKV_IN = 3 * BRANCH_W + RWKV_W_LORA + RWKV_A_LORA + RWKV_G_LORA
SCAN_ROWS = 256
Q_TILE = 128
FFN_ROW_PARTS = 2
PAGE_GROUP = 64
LANE = 128

Z_GATE = 0
Z_POOL = N_BRANCH * D_MODEL
Z_MQ = Z_POOL + BRANCH_W
Z_MKV = Z_MQ + MLA_Q_LORA
Z_NQ = Z_MKV + MLA_ROW_W
Z_NKV = Z_NQ + BRANCH_W
Z_NG = Z_NKV + 6 * NSA_HD
Z_RW = Z_NG + LANE
Z_END = Z_RW + RWKV_IN
Z_TILE = 512
Z_W = -(-Z_END // Z_TILE) * Z_TILE


def _cparams(sem, vmem_mb=48):
    return pltpu.CompilerParams(dimension_semantics=sem, vmem_limit_bytes=vmem_mb * 1024 * 1024)


def _pick_tile(n, cands):
    for c in cands:
        if n % c == 0:
            return c
    raise ValueError(f"no tile in {cands} divides {n}")


def _vec3(a):
    return a.reshape(a.shape[0], 1, a.shape[1])


def _vec_spec(w, l, ngrid):
    if ngrid == 1:
        return pl.BlockSpec((None, 1, w), lambda i: (l, 0, 0))
    return pl.BlockSpec((None, 1, w), lambda i, j: (l, 0, 0))


def _dot(a, b):
    return jnp.dot(a.astype(BF16), b.astype(BF16), preferred_element_type=F32)


def _dot_nt(a, b):
    return lax.dot_general(a.astype(BF16), b.astype(BF16), (((1,), (1,)), ((), ())),
                           preferred_element_type=F32)


def _dot_split(a, b01, terms=2):
    acc = None
    rem = a
    for _ in range(terms):
        piece = rem.astype(BF16)
        d = jnp.dot(piece, b01, preferred_element_type=F32)
        acc = d if acc is None else acc + d
        rem = rem - piece.astype(F32)
    return acc


def _rms(x, g):
    return x * lax.rsqrt(jnp.mean(x * x, -1, keepdims=True) + RMS_EPS) * g


def _rope_apply(x, cos, sin):
    w = x.shape[-1]
    lane = lax.broadcasted_iota(jnp.int32, x.shape, x.ndim - 1)
    fwd = pltpu.roll(x, w - 32, x.ndim - 1)
    bwd = pltpu.roll(x, 32, x.ndim - 1)
    return x * cos + jnp.where((lane & 63) < 32, fwd, bwd) * sin


def _ffn_body(x_ref, g_ref, wg_ref, wu_ref, wd_ref, gf_ref, o_ref, h_ref, acc_ref, *, nj, final_norm):
    j = pl.program_id(1)

    @pl.when(j == 0)
    def _():
        h_ref[...] = _rms(x_ref[...], g_ref[...]).astype(BF16)
        acc_ref[...] = jnp.zeros(acc_ref.shape, F32)

    part = x_ref.shape[0] // FFN_ROW_PARTS
    for r in [pl.ds(k * part, part) for k in range(FFN_ROW_PARTS)]:
        h = h_ref[r]
        gt = jnp.dot(h, wg_ref[...], preferred_element_type=F32)
        up = jnp.dot(h, wu_ref[...], preferred_element_type=F32)
        acc_ref[r] += _dot(gt * jax.nn.sigmoid(gt) * up, wd_ref[...])

    @pl.when(j == nj - 1)
    def _():
        y = x_ref[...] + 0.5 * acc_ref[...]
        if final_norm:
            y = _rms(y, gf_ref[...])
        o_ref[...] = y


def _col_tiles(w, t):
    layers, d, f = w.shape
    return w.reshape(layers, d, f // t, t).transpose(0, 2, 1, 3)


def _ffn(x, g, wg, wu, wd, gf, l, final_norm):
    n, d = x.shape
    nj, tf = wg.shape[1], wg.shape[3]
    tm = _pick_tile(n, (512, 256, 128))
    return pl.pallas_call(
        functools.partial(_ffn_body, nj=nj, final_norm=final_norm),
        grid=(n // tm, nj),
        in_specs=[
            pl.BlockSpec((tm, d), lambda i, j: (i, 0)),
            pl.BlockSpec((1, d), lambda i, j: (0, 0)),
            pl.BlockSpec((None, None, d, tf), lambda i, j: (l, j, 0, 0)),
            pl.BlockSpec((None, None, d, tf), lambda i, j: (l, j, 0, 0)),
            pl.BlockSpec((None, tf, d), lambda i, j: (l, j, 0)),
            pl.BlockSpec((1, d), lambda i, j: (0, 0)),
        ],
        out_specs=pl.BlockSpec((tm, d), lambda i, j: (i, 0)),
        out_shape=jax.ShapeDtypeStruct((n, d), F32),
        scratch_shapes=[pltpu.VMEM((tm, d), BF16), pltpu.VMEM((tm, d), F32)],
        compiler_params=_cparams(("parallel", "arbitrary")),
        name="ffn",
    )(x, g.reshape(1, d), wg, wu, wd, gf.reshape(1, d))


def _inproj_body(x_ref, g_ref, w_ref, o_ref, h_ref):
    @pl.when(pl.program_id(1) == 0)
    def _():
        h_ref[...] = _rms(x_ref[...], g_ref[...]).astype(BF16)

    o_ref[...] = jnp.dot(h_ref[...], w_ref[...], preferred_element_type=F32)


def _inproj(x, g, w, l):
    n, d = x.shape
    zw = w.shape[1] * Z_TILE
    tm = _pick_tile(n, (1024, 512, 256, 128))
    return pl.pallas_call(
        _inproj_body,
        grid=(n // tm, zw // Z_TILE),
        in_specs=[
            pl.BlockSpec((tm, d), lambda i, j: (i, 0)),
            pl.BlockSpec((1, d), lambda i, j: (0, 0)),
            pl.BlockSpec((None, None, d, Z_TILE), lambda i, j: (l, j, 0, 0)),
        ],
        out_specs=pl.BlockSpec((tm, Z_TILE), lambda i, j: (i, j)),
        out_shape=jax.ShapeDtypeStruct((n, zw), F32),
        scratch_shapes=[pltpu.VMEM((tm, d), BF16)],
        compiler_params=_cparams(("parallel", "arbitrary")),
        name="inproj",
    )(x, g.reshape(1, d), w)


def _merge_body(x_ref, br_ref, zg_ref, wb_ref, wo_ref, o_ref, acc_ref):
    n = pl.program_id(1)
    t = jax.nn.sigmoid(zg_ref[...]) * _dot(br_ref[...], wb_ref[...])

    @pl.when(n == 0)
    def _():
        acc_ref[...] = t

    @pl.when(n > 0)
    def _():
        acc_ref[...] += t

    @pl.when(n == N_BRANCH - 1)
    def _():
        o_ref[...] = x_ref[...] + _dot(acc_ref[...], wo_ref[...])


def _merge(x, branches, z, wb, wo, l):
    n, d = x.shape
    tm = _pick_tile(n, (256, 128))
    return pl.pallas_call(
        _merge_body,
        grid=(n // tm, N_BRANCH),
        in_specs=[
            pl.BlockSpec((tm, d), lambda i, k: (i, 0)),
            pl.BlockSpec((tm, BRANCH_W), lambda i, k: (i, k)),
            pl.BlockSpec((tm, d), lambda i, k: (i, k)),
            pl.BlockSpec((None, None, BRANCH_W, d), lambda i, k: (l, k, 0, 0)),
            pl.BlockSpec((None, d, d), lambda i, k: (l, 0, 0)),
        ],
        out_specs=pl.BlockSpec((tm, d), lambda i, k: (i, 0)),
        out_shape=jax.ShapeDtypeStruct((n, d), F32),
        scratch_shapes=[pltpu.VMEM((tm, d), F32)],
        compiler_params=_cparams(("parallel", "arbitrary")),
        name="merge",
    )(x, branches, z, wb, wo)


def _pool_body(u_ref, w_ref, sc_ref, o_ref, *, t_len, t_chunk, pos0):
    g = pl.program_id(1)
    bt = u_ref.shape[0]
    for gi, win in enumerate(POOL_WINDOWS):
        @pl.when(g == gi)
        def _(win=win):
            for c0 in range(0, t_len, t_chunk):
                ue = u_ref[:, c0:c0 + t_chunk + POOL_HALO, :]
                acc = ue
                span = 1
                while span < win:
                    acc = acc[:, span:, :] + acc[:, :-span, :]
                    span *= 2
                wsum = acc[:, POOL_HALO + 1 - win:POOL_HALO + 1 - win + t_chunk, :]
                pos = pos0 + c0 + lax.broadcasted_iota(jnp.int32, (1, t_chunk, 1), 1)
                cnt = jnp.minimum(win, pos + 1).astype(F32)
                diff = wsum / cnt - ue[:, POOL_HALO:, :]
                y = _dot(diff.reshape(bt * t_chunk, LANE), w_ref[...]) * sc_ref[...]
                o_ref[:, c0:c0 + t_chunk, :] = y.reshape(bt, t_chunk, LANE)


def _pool(u_ext, w, scale, l, pos0, bt):
    b, te, c = u_ext.shape
    t_len = te - POOL_HALO
    t_chunk = _pick_tile(t_len, (512, 256, 128, 8))
    return pl.pallas_call(
        functools.partial(_pool_body, t_len=t_len, t_chunk=t_chunk, pos0=pos0),
        grid=(b // bt, len(POOL_WINDOWS)),
        in_specs=[
            pl.BlockSpec((bt, te, LANE), lambda i, g: (i, 0, g)),
            pl.BlockSpec((None, None, POOL_GW, POOL_GW), lambda i, g: (l, g, 0, 0)),
            pl.BlockSpec((None, 1, LANE), lambda i, g: (l, 0, g)),
        ],
        out_specs=pl.BlockSpec((bt, t_len, LANE), lambda i, g: (i, 0, g)),
        out_shape=jax.ShapeDtypeStruct((b, t_len, c), F32),
        compiler_params=_cparams(("parallel", "arbitrary")),
        name="pool",
    )(u_ext, w, _vec3(scale))


def _mla_proj_body(zq_ref, zkv_ref, qn_ref, kvn_ref, wuq_ref, wuk_ref, cq_ref, sq_ref, ckv_ref, skv_ref,
                   qcat_ref, qcat_t_ref, rows_ref, rowsb_ref, rows_t_ref):
    tm = zq_ref.shape[0]
    zk = zkv_ref[...]
    ckv = _rms(zk[:, :MLA_KV_LORA], kvn_ref[...])
    roped = _rope_apply(zk, ckv_ref[...], skv_ref[...])
    rows = jnp.concatenate([ckv, roped[:, MLA_KV_LORA:]], -1)
    rows_ref[...] = rows
    rowsb_ref[...] = rows.astype(BF16)
    rows_t_ref[0] = rows.T.astype(BF16)
    q = _dot(_rms(zq_ref[...], qn_ref[...]), wuq_ref[...])
    q = _rope_apply(q, cq_ref[...], sq_ref[...])
    pad = jnp.zeros((tm, MLA_ROW_W - MLA_CACHE_W), F32)
    for h in range(MLA_HEADS):
        q_lat = _dot(q[:, h * MLA_NOPE:(h + 1) * MLA_NOPE], wuk_ref[h])
        q_pe = q[:, MLA_HEADS * MLA_NOPE + h * MLA_ROPE:MLA_HEADS * MLA_NOPE + (h + 1) * MLA_ROPE]
        qcat = jnp.concatenate([q_lat, q_pe, pad], -1) * MLA_SCALE
        qcat_ref[0, h] = qcat.astype(BF16)
        qcat_t_ref[0, :, h * tm:(h + 1) * tm] = qcat.T.astype(BF16)


def _mla_proj(zq, zkv, qn, kvn, wuq, wuk, tabs, l):
    n = zq.shape[0]
    tm = Q_TILE
    qw = wuq.shape[2]
    row = lambda w: pl.BlockSpec((tm, w), lambda i: (i, 0))
    return pl.pallas_call(
        _mla_proj_body,
        grid=(n // tm,),
        in_specs=[
            row(MLA_Q_LORA), row(MLA_ROW_W),
            _vec_spec(MLA_Q_LORA, l, 1), _vec_spec(MLA_KV_LORA, l, 1),
            pl.BlockSpec((None, MLA_Q_LORA, qw), lambda i: (l, 0, 0)),
            pl.BlockSpec((None, MLA_HEADS, MLA_NOPE, MLA_KV_LORA), lambda i: (l, 0, 0, 0)),
            row(qw), row(qw), row(MLA_ROW_W), row(MLA_ROW_W),
        ],
        out_specs=[
            pl.BlockSpec((1, MLA_HEADS, tm, MLA_ROW_W), lambda i: (i, 0, 0, 0)),
            pl.BlockSpec((1, MLA_ROW_W, MLA_HEADS * tm), lambda i: (i, 0, 0)),
            row(MLA_ROW_W), row(MLA_ROW_W),
            pl.BlockSpec((1, MLA_ROW_W, tm), lambda i: (i, 0, 0)),
        ],
        out_shape=[
            jax.ShapeDtypeStruct((n // tm, MLA_HEADS, tm, MLA_ROW_W), BF16),
            jax.ShapeDtypeStruct((n // tm, MLA_ROW_W, MLA_HEADS * tm), BF16),
            jax.ShapeDtypeStruct((n, MLA_ROW_W), F32),
            jax.ShapeDtypeStruct((n, MLA_ROW_W), BF16),
            jax.ShapeDtypeStruct((n // tm, MLA_ROW_W, tm), BF16),
        ],
        compiler_params=_cparams(("parallel",)),
        name="mla_proj",
    )(zq, zkv, _vec3(qn), _vec3(kvn), wuq, wuk, *tabs)


def _masked(x, mask, heads, fill):
    if mask is None:
        return x
    tq, tk = mask.shape
    return jnp.where(mask[None], x.reshape(heads, tq, tk), fill).reshape(heads * tq, tk)


def _softmax_update(state, s, mask, heads, v, vt=False):
    m_old, l_old, acc_old = state
    s = _masked(s, mask, heads, NEG)
    m_new = jnp.maximum(m_old, jnp.max(s, -1, keepdims=True))
    alpha = jnp.exp(m_old - m_new)
    p = _masked(jnp.exp(s - m_new), mask, heads, 0.0)
    return (m_new, alpha * l_old + jnp.sum(p, -1, keepdims=True),
            alpha * acc_old + (_dot_nt(p, v) if vt else _dot(p, v)))


def _softmax_step(s, mask, heads, m_ref, l_ref, acc_ref, v, vt=False):
    m_ref[...], l_ref[...], acc_ref[...] = _softmax_update((m_ref[...], l_ref[...], acc_ref[...]), s, mask, heads,
                                                           v, vt)


def _softmax_once(s, mask, heads, v, vt=False):
    s = _masked(s, mask, heads, NEG)
    e = _masked(jnp.exp(s - jnp.max(s, -1, keepdims=True)), mask, heads, 0.0)
    den = jnp.sum(e, -1, keepdims=True)
    p = e / jnp.where(den > 0, den, 1.0)
    return (_dot_nt(p, v) if vt else _dot(p, v)), p


def _lanes(a, heads):
    return jnp.concatenate([a] * heads, 1)


def _softmax_step_t(st, bias, m_ref, l_ref, acc_ref, vt):
    if bias is not None:
        st = st + bias
    m_old = m_ref[...]
    m_new = jnp.maximum(m_old, jnp.max(st, 0, keepdims=True))
    alpha = jnp.exp(m_old - m_new)
    p = jnp.exp(st - m_new)
    l_ref[...] = alpha * l_ref[...] + jnp.sum(p, 0, keepdims=True)
    acc_ref[...] = alpha * acc_ref[...] + _dot(vt, p)
    m_ref[...] = m_new


def _softmax_once_t(st, keep, vt):
    keep = keep > 0.5
    st = jnp.where(keep, st, NEG)
    e = jnp.where(keep, jnp.exp(st - jnp.max(st, 0, keepdims=True)), 0.0)
    den = jnp.sum(e, 0, keepdims=True)
    p = e / jnp.where(den > 0, den, 1.0)
    return _dot(vt, p), p


def _softmax_init(m_ref, l_ref, acc_ref):
    m_ref[...] = jnp.full(m_ref.shape, NEG, F32)
    l_ref[...] = jnp.zeros(l_ref.shape, F32)
    acc_ref[...] = jnp.zeros(acc_ref.shape, F32)


def _softmax_out(l_ref, acc_ref):
    l = l_ref[...]
    return acc_ref[...] / jnp.where(l > 0, l, 1.0)


def _mla_prompt_body(qt_ref, k_ref, kt_ref, wuv_ref, o_ref, m_ref, l_ref, acc_ref):
    i = pl.program_id(1)
    tq = kt_ref.shape[2]
    qt = qt_ref[0]
    _softmax_init(m_ref, l_ref, acc_ref)

    def tile(j):
        return k_ref[0, pl.ds(pl.multiple_of(j * tq, tq), tq), :], kt_ref[j, :MLA_KV_LORA, :]

    def pair_step(jj, carry):
        k = k_ref[0, pl.ds(pl.multiple_of(jj * 2 * tq, 2 * tq), 2 * tq), :]
        vt = jnp.concatenate([kt_ref[2 * jj, :MLA_KV_LORA, :], kt_ref[2 * jj + 1, :MLA_KV_LORA, :]], 1)
        _softmax_step_t(_dot(k, qt), None, m_ref, l_ref, acc_ref, vt)
        return carry

    lax.fori_loop(0, i // 2, pair_step, 0)

    @pl.when(i % 2 == 1)
    def _():
        k, vt = tile(i - 1)
        _softmax_step_t(_dot(k, qt), None, m_ref, l_ref, acc_ref, vt)

    k, vt = tile(i)
    causal = (lax.broadcasted_iota(jnp.int32, (tq, tq), 0) <= lax.broadcasted_iota(jnp.int32, (tq, tq), 1))
    _softmax_step_t(_dot(k, qt), _lanes(jnp.where(causal, 0.0, MASK_BIAS), MLA_HEADS), m_ref, l_ref, acc_ref, vt)
    o_lat_t = acc_ref[...] / l_ref[...]
    for h in range(MLA_HEADS):
        o_ref[:, h * MLA_V:(h + 1) * MLA_V] = _dot(o_lat_t[:, h * tq:(h + 1) * tq].T, wuv_ref[h])


def _mla_prompt(qcat_t, rows_b, rows_t, wuv, l, b, s):
    tq = Q_TILE
    nq = s // tq
    rows = MLA_HEADS * tq
    return pl.pallas_call(
        _mla_prompt_body,
        grid=(b, nq),
        in_specs=[
            pl.BlockSpec((1, MLA_ROW_W, rows), lambda bi, i: (bi * nq + i, 0, 0)),
            pl.BlockSpec((1, s, MLA_ROW_W), lambda bi, i: (bi, 0, 0)),
            pl.BlockSpec((nq, MLA_ROW_W, tq), lambda bi, i: (bi, 0, 0)),
            pl.BlockSpec((None, MLA_HEADS, MLA_KV_LORA, MLA_V), lambda bi, i: (l, 0, 0, 0)),
        ],
        out_specs=pl.BlockSpec((tq, BRANCH_W), lambda bi, i: (bi * nq + i, 0)),
        out_shape=jax.ShapeDtypeStruct((b * s, BRANCH_W), F32),
        scratch_shapes=[pltpu.VMEM((1, rows), F32), pltpu.VMEM((1, rows), F32),
                        pltpu.VMEM((MLA_KV_LORA, rows), F32)],
        compiler_params=_cparams(("parallel", "arbitrary")),
        name="mla_prompt",
    )(qcat_t, rows_b.reshape(b, s, MLA_ROW_W), rows_t, wuv)


def _mla_sample_body(pt_ref, q_ref, *refs, pc, t_new):
    page_refs = refs[:pc]
    new_ref, wuv_ref, o_ref, m_ref, l_ref, acc_ref = refs[pc:]
    c = pl.program_id(1)

    @pl.when(c == 0)
    def _():
        _softmax_init(m_ref, l_ref, acc_ref)

    q = q_ref[0][:, :MLA_CACHE_W]
    state = (m_ref[...], l_ref[...], acc_ref[...])
    for g in range(0, pc, PAGE_GROUP):
        keys_t = jnp.concatenate([r[...].astype(BF16) for r in page_refs[g:g + PAGE_GROUP]], 1)
        state = _softmax_update(state, _dot(q, keys_t), None, MLA_HEADS, keys_t[:MLA_KV_LORA], vt=True)
    m_ref[...], l_ref[...], acc_ref[...] = state

    @pl.when(c == pl.num_programs(1) - 1)
    def _():
        k_new = new_ref[0][:, :MLA_CACHE_W]
        tq = lax.broadcasted_iota(jnp.int32, (t_new, t_new), 0)
        tk = lax.broadcasted_iota(jnp.int32, (t_new, t_new), 1)
        _softmax_step(_dot_nt(q, k_new), tk <= tq, MLA_HEADS, m_ref, l_ref, acc_ref, k_new[:, :MLA_KV_LORA])
        o_lat = _softmax_out(l_ref, acc_ref)
        for h in range(MLA_HEADS):
            o_ref[0, :, h * MLA_V:(h + 1) * MLA_V] = _dot(o_lat[h * t_new:(h + 1) * t_new], wuv_ref[h])


def _page_chunk(n_pages):
    return _pick_tile(n_pages, (64, 32, 16, 8, 4, 2, 1))


def _mla_sample(page_table, q, cache_t, rows_new, wuv, l):
    db, rows, _ = q.shape
    t_new = rows // MLA_HEADS
    n_pages = page_table.shape[1]
    page = cache_t.shape[3]
    pc = _page_chunk(n_pages)

    def page_spec(k):
        return pl.BlockSpec((None, None, MLA_CACHE_W, page),
                            lambda bi, c, pt: (l, pt[bi, c * pc + k], 0, 0))

    grid_spec = pltpu.PrefetchScalarGridSpec(
        num_scalar_prefetch=1,
        grid=(db, n_pages // pc),
        in_specs=[pl.BlockSpec((1, rows, MLA_ROW_W), lambda bi, c, pt: (bi, 0, 0))]
        + [page_spec(k) for k in range(pc)]
        + [pl.BlockSpec((1, t_new, MLA_ROW_W), lambda bi, c, pt: (bi, 0, 0)),
           pl.BlockSpec((None, MLA_HEADS, MLA_KV_LORA, MLA_V), lambda bi, c, pt: (l, 0, 0, 0))],
        out_specs=pl.BlockSpec((1, t_new, BRANCH_W), lambda bi, c, pt: (bi, 0, 0)),
        scratch_shapes=[pltpu.VMEM((rows, 1), F32), pltpu.VMEM((rows, 1), F32),
                        pltpu.VMEM((rows, MLA_KV_LORA), F32)],
    )
    return pl.pallas_call(
        functools.partial(_mla_sample_body, pc=pc, t_new=t_new),
        grid_spec=grid_spec,
        out_shape=jax.ShapeDtypeStruct((db, t_new, BRANCH_W), F32),
        compiler_params=_cparams(("parallel", "arbitrary")),
        name="mla_sample",
    )(page_table, q, *([cache_t] * pc), rows_new, wuv)


def _nsa_proj_body(zq_ref, zkv_ref, zg_ref, cq_ref, sq_ref, ckv_ref, skv_ref, q_ref, qt_ref, kv_ref, g_ref):
    tm = zq_ref.shape[0]
    q = _rope_apply(zq_ref[...], cq_ref[...], sq_ref[...]) * NSA_SCALE
    q_t = q.T
    for h in range(NSA_HEADS):
        q_ref[0, h] = q[:, h * NSA_HD:(h + 1) * NSA_HD].astype(BF16)
        qt_ref[0, :, h * tm:(h + 1) * tm] = q_t[h * NSA_HD:(h + 1) * NSA_HD].astype(BF16)
    kv_ref[...] = _rope_apply(zkv_ref[...], ckv_ref[...], skv_ref[...])
    g_ref[...] = jax.nn.sigmoid(zg_ref[...])


def _nsa_proj(zq, zkv, zg, tabs):
    n = zq.shape[0]
    tm = Q_TILE
    kvw = zkv.shape[1]
    row = lambda w: pl.BlockSpec((tm, w), lambda i: (i, 0))
    return pl.pallas_call(
        _nsa_proj_body,
        grid=(n // tm,),
        in_specs=[row(BRANCH_W), row(kvw), row(LANE), row(BRANCH_W), row(BRANCH_W), row(kvw), row(kvw)],
        out_specs=[pl.BlockSpec((1, NSA_HEADS, tm, NSA_HD), lambda i: (i, 0, 0, 0)),
                   pl.BlockSpec((1, NSA_HD, NSA_HEADS * tm), lambda i: (i, 0, 0)), row(kvw), row(LANE)],
        out_shape=[
            jax.ShapeDtypeStruct((n // tm, NSA_HEADS, tm, NSA_HD), BF16),
            jax.ShapeDtypeStruct((n // tm, NSA_HD, NSA_HEADS * tm), BF16),
            jax.ShapeDtypeStruct((n, kvw), F32),
            jax.ShapeDtypeStruct((n, LANE), F32),
        ],
        compiler_params=_cparams(("parallel",)),
        name="nsa_proj",
    )(zq, zkv, zg, *tabs)


def _bmean_prompt_body(kv_ref, o_ref):
    nb = o_ref.shape[1]
    x = kv_ref[0][:nb * NSA_BLOCK]
    o_ref[0] = jnp.sum(x.reshape(nb, NSA_BLOCK, LANE), 1) * (1.0 / NSA_BLOCK)


def _bmean_prompt(kv, b, s):
    nb = s // NSA_BLOCK
    return pl.pallas_call(
        _bmean_prompt_body,
        grid=(b,),
        in_specs=[pl.BlockSpec((1, s, LANE), lambda bi: (bi, 0, 0))],
        out_specs=pl.BlockSpec((1, nb, LANE), lambda bi: (bi, 0, 0)),
        out_shape=jax.ShapeDtypeStruct((b, nb, LANE), F32),
        compiler_params=_cparams(("parallel",)),
        name="bmean_prompt",
    )(kv.reshape(b, s, kv.shape[-1]))


def _bmean_sample_body(pt_ref, *refs, pc):
    a_ref, o_ref = refs[pc], refs[pc + 1]
    x = jnp.concatenate([r[...].astype(BF16) for r in refs[:pc]], 1)
    o_ref[0, 0] = jnp.dot(x, a_ref[...], preferred_element_type=F32) * (1.0 / NSA_BLOCK)


def _bmean_sample(page_table, cache_t, l):
    db, n_pages = page_table.shape
    page = cache_t.shape[3]
    pc = _page_chunk(n_pages)
    per = page // NSA_BLOCK
    nch = n_pages // pc
    ind = (jnp.arange(pc * page)[:, None] // NSA_BLOCK == jnp.arange(pc * per)[None, :]).astype(BF16)

    def page_spec(k):
        return pl.BlockSpec((None, None, 2 * NSA_HD, page), lambda bi, c, pt: (l, pt[bi, c * pc + k], 0, 0))

    grid_spec = pltpu.PrefetchScalarGridSpec(
        num_scalar_prefetch=1,
        grid=(db, nch),
        in_specs=[page_spec(k) for k in range(pc)]
        + [pl.BlockSpec((pc * page, pc * per), lambda bi, c, pt: (0, 0))],
        out_specs=pl.BlockSpec((1, 1, 2 * NSA_HD, pc * per), lambda bi, c, pt: (bi, c, 0, 0)),
    )
    out = pl.pallas_call(
        functools.partial(_bmean_sample_body, pc=pc),
        grid_spec=grid_spec,
        out_shape=jax.ShapeDtypeStruct((db, nch, 2 * NSA_HD, pc * per), F32),
        compiler_params=_cparams(("parallel", "arbitrary")),
        name="bmean_sample",
    )(page_table, *([cache_t] * pc), ind)
    return out.transpose(0, 2, 1, 3).reshape(db, 2 * NSA_HD, n_pages * per)


def _cmp_proj_t_body(x_ref, phi_ref, c_ref, s_ref, o_ref):
    half = NSA_HD // 2
    for bi in range(x_ref.shape[0]):
        y = _dot(phi_ref[...], x_ref[bi])
        yk = y[:NSA_HD]
        swapped = jnp.concatenate([yk[half:], yk[:half]], 0)
        o_ref[bi] = jnp.concatenate([yk * c_ref[...] + swapped * s_ref[...], y[NSA_HD:]], 0)


def _cmp_proj_t(means_t, phi_t, cos_t, sin_t, l):
    b, _, nb = means_t.shape
    bt = _pick_tile(b, (8, 4, 2, 1))
    return pl.pallas_call(
        _cmp_proj_t_body,
        grid=(b // bt,),
        in_specs=[
            pl.BlockSpec((bt, LANE, nb), lambda i: (i, 0, 0)),
            pl.BlockSpec((None, LANE, LANE), lambda i: (l, 0, 0)),
            pl.BlockSpec((NSA_HD, nb), lambda i: (0, 0)),
            pl.BlockSpec((NSA_HD, nb), lambda i: (0, 0)),
        ],
        out_specs=pl.BlockSpec((bt, LANE, nb), lambda i: (i, 0, 0)),
        out_shape=jax.ShapeDtypeStruct((b, LANE, nb), F32),
        compiler_params=_cparams(("parallel",)),
        name="cmp_proj_t",
    )(means_t, phi_t, cos_t, sin_t)


def _cmp_proj_body(x_ref, phi_ref, c_ref, s_ref, o_ref):
    bt, nb, _ = x_ref.shape
    y = _dot(x_ref[...].reshape(bt * nb, LANE), phi_ref[...]).reshape(bt, nb, LANE)
    o_ref[...] = _rope_apply(y, c_ref[...][None], s_ref[...][None])


def _cmp_proj(means, phi, cos, sin, l):
    b, nb, _ = means.shape
    bt = _pick_tile(b, (8, 4, 2, 1))
    return pl.pallas_call(
        _cmp_proj_body,
        grid=(b // bt,),
        in_specs=[
            pl.BlockSpec((bt, nb, LANE), lambda i: (i, 0, 0)),
            pl.BlockSpec((None, LANE, LANE), lambda i: (l, 0, 0)),
            pl.BlockSpec((nb, LANE), lambda i: (0, 0)),
            pl.BlockSpec((nb, LANE), lambda i: (0, 0)),
        ],
        out_specs=pl.BlockSpec((bt, nb, LANE), lambda i: (i, 0, 0)),
        out_shape=jax.ShapeDtypeStruct((b, nb, LANE), F32),
        compiler_params=_cparams(("parallel",)),
        name="cmp_proj",
    )(means, phi, cos, sin)


def _topk_select(imp, cand, k, axis):
    nb = imp.shape[axis]
    idx = lax.broadcasted_iota(jnp.int32, imp.shape, axis).astype(F32)
    v = imp if cand is None else jnp.where(cand, imp, -1.0)
    sel = jnp.zeros(imp.shape, F32)
    for _ in range(k):
        m = jnp.max(v, axis, keepdims=True)
        first = jnp.min(jnp.where(v == m, idx, float(nb)), axis, keepdims=True)
        hit = idx == first
        sel = jnp.where(hit & (m >= 0.0), 1.0, sel)
        v = jnp.where(hit, -2.0, v)
    return sel


def _nsa_prompt_body(qt_ref, kbvb_ref, kv_ref, g_ref, o_ref, m_ref, l_ref, acc_ref):
    i = pl.program_id(1)
    heads = NSA_HEADS
    tq = qt_ref.shape[2] // heads
    nb = kbvb_ref.shape[1]
    qt = qt_ref[0]
    kbvb = kbvb_ref[0]

    blk = lax.broadcasted_iota(jnp.int32, (nb, tq), 0)
    qpos_b = i * tq + lax.broadcasted_iota(jnp.int32, (nb, tq), 1)
    vis = jnp.where(blk * NSA_BLOCK + (NSA_BLOCK - 1) <= qpos_b, 1.0, 0.0)
    o_cmp, p = _softmax_once_t(_dot(kbvb[:, :NSA_HD], qt), _lanes(vis, heads), kbvb[:, NSA_HD:].T)
    imp = p[:, :tq]
    for h in range(1, heads):
        imp = imp + p[:, h * tq:(h + 1) * tq]
    cur = qpos_b // NSA_BLOCK
    sel = jnp.where(blk == cur, 1.0, _topk_select(imp, blk < cur, min(NSA_TOPN - 1, nb), 0))

    koff = lax.broadcasted_iota(jnp.int32, (tq, tq), 0)
    qpos = i * tq + lax.broadcasted_iota(jnp.int32, (tq, tq), 1)
    per = tq // NSA_BLOCK
    e_key = lax.broadcasted_iota(jnp.int32, (tq, nb), 0) // NSA_BLOCK
    e_blk = lax.broadcasted_iota(jnp.int32, (tq, nb), 1)

    def sel_step(j, carry):
        kv = kv_ref[0, pl.ds(pl.multiple_of(j * tq, tq), tq), 2 * NSA_HD:4 * NSA_HD]
        chosen = _dot(jnp.where(e_blk == j * per + e_key, 1.0, 0.0), sel)
        bias = jnp.where((chosen > 0.5) & (j * tq + koff <= qpos), 0.0, MASK_BIAS)
        _softmax_step_t(_dot(kv[:, :NSA_HD], qt), _lanes(bias, heads), m_ref, l_ref, acc_ref, kv[:, NSA_HD:].T)
        return carry

    _softmax_init(m_ref, l_ref, acc_ref)
    lax.fori_loop(0, i + 1, sel_step, 0)
    o_sel = acc_ref[...] / l_ref[...]

    def win_step(j, carry):
        kv = kv_ref[0, pl.ds(pl.multiple_of(j * tq, tq), tq), 4 * NSA_HD:6 * NSA_HD]
        rel = qpos - (j * tq + koff)
        bias = jnp.where((rel >= 0) & (rel < NSA_WINDOW), 0.0, MASK_BIAS)
        _softmax_step_t(_dot(kv[:, :NSA_HD], qt), _lanes(bias, heads), m_ref, l_ref, acc_ref, kv[:, NSA_HD:].T)
        return carry

    _softmax_init(m_ref, l_ref, acc_ref)
    lax.fori_loop(jnp.maximum(i - NSA_WINDOW // tq, 0), i + 1, win_step, 0)
    o_win = acc_ref[...] / l_ref[...]

    g_t = g_ref[...].T
    outs = []
    for h in range(heads):
        cols = slice(h * tq, (h + 1) * tq)
        outs.append(g_t[h:h + 1] * o_cmp[:, cols] + g_t[heads + h:heads + h + 1] * o_sel[:, cols]
                    + g_t[2 * heads + h:2 * heads + h + 1] * o_win[:, cols])
    o_ref[...] = jnp.concatenate(outs, 0).T


def _nsa_prompt(q_t, kbvb, kv, gates, b, s):
    tq = Q_TILE
    nq = s // tq
    nb = kbvb.shape[1]
    kvw = kv.shape[-1]
    rows = NSA_HEADS * tq
    return pl.pallas_call(
        _nsa_prompt_body,
        grid=(b, nq),
        in_specs=[
            pl.BlockSpec((1, NSA_HD, rows), lambda bi, i: (bi * nq + i, 0, 0)),
            pl.BlockSpec((1, nb, LANE), lambda bi, i: (bi, 0, 0)),
            pl.BlockSpec((1, s, kvw), lambda bi, i: (bi, 0, 0)),
            pl.BlockSpec((tq, LANE), lambda bi, i: (bi * nq + i, 0)),
        ],
        out_specs=pl.BlockSpec((tq, BRANCH_W), lambda bi, i: (bi * nq + i, 0)),
        out_shape=jax.ShapeDtypeStruct((b * s, BRANCH_W), F32),
        scratch_shapes=[pltpu.VMEM((1, rows), F32), pltpu.VMEM((1, rows), F32), pltpu.VMEM((NSA_HD, rows), F32)],
        compiler_params=_cparams(("parallel", "arbitrary")),
        name="nsa_prompt",
    )(q_t, kbvb, kv.reshape(b, s, kvw), gates)


def _nsa_cmp_sample_body(q_ref, kt_ref, o_ref, sel_ref, *, t_new):
    bt, _, nb = kt_ref.shape
    imps = []
    for bi in range(bt):
        o, p = _softmax_once(_dot(q_ref[bi], kt_ref[bi, :NSA_HD, :]), None, NSA_HEADS, kt_ref[bi, NSA_HD:, :],
                             vt=True)
        o_ref[bi] = o
        imps.append(jnp.sum(p.reshape(NSA_HEADS, t_new, nb), 0))
    sel = _topk_select(jnp.concatenate(imps, 0), None, min(NSA_TOPN - 1, nb), 1)
    sel_ref[...] = sel.reshape(bt, t_new, nb)


def _nsa_cmp_sample(q, kbvb_t, t_new):
    db, rows, _ = q.shape
    nb = kbvb_t.shape[2]
    bt = _pick_tile(db, (8, 4, 2, 1))
    return pl.pallas_call(
        functools.partial(_nsa_cmp_sample_body, t_new=t_new),
        grid=(db // bt,),
        in_specs=[pl.BlockSpec((bt, rows, NSA_HD), lambda bi: (bi, 0, 0)),
                  pl.BlockSpec((bt, LANE, nb), lambda bi: (bi, 0, 0))],
        out_specs=[pl.BlockSpec((bt, rows, NSA_HD), lambda bi: (bi, 0, 0)),
                   pl.BlockSpec((bt, t_new, nb), lambda bi: (bi, 0, 0))],
        out_shape=[jax.ShapeDtypeStruct((db, rows, NSA_HD), F32), jax.ShapeDtypeStruct((db, t_new, nb), F32)],
        compiler_params=_cparams(("parallel",)),
        name="nsa_cmp_sample",
    )(q, kbvb_t)


def _nsa_sel_sample_body(pt_ref, q_ref, sel_ref, exp_ref, *refs, pc, t_new):
    page_refs = refs[:pc]
    new_ref, o_ref, m_ref, l_ref, acc_ref = refs[pc:]
    c = pl.program_id(1)

    @pl.when(c == 0)
    def _():
        _softmax_init(m_ref, l_ref, acc_ref)

    q = q_ref[0]
    page = page_refs[0].shape[1]
    chosen = _dot(sel_ref[0, 0], exp_ref[...])
    state = (m_ref[...], l_ref[...], acc_ref[...])
    for g in range(0, pc, PAGE_GROUP):
        kv_t = jnp.concatenate([r[...].astype(BF16) for r in page_refs[g:g + PAGE_GROUP]], 1)
        state = _softmax_update(state, _dot(q, kv_t[:NSA_HD]), chosen[:, g * page:(g + PAGE_GROUP) * page] > 0.5,
                                NSA_HEADS, kv_t[NSA_HD:], vt=True)
    m_ref[...], l_ref[...], acc_ref[...] = state

    @pl.when(c == pl.num_programs(1) - 1)
    def _():
        kv_new = new_ref[0]
        tq = lax.broadcasted_iota(jnp.int32, (t_new, t_new), 0)
        tk = lax.broadcasted_iota(jnp.int32, (t_new, t_new), 1)
        _softmax_step(_dot_nt(q, kv_new[:, :NSA_HD]), tk <= tq, NSA_HEADS, m_ref, l_ref, acc_ref,
                      kv_new[:, NSA_HD:])
        o_ref[0] = _softmax_out(l_ref, acc_ref)


def _nsa_sel_sample(page_table, q, sel, cache_t, kv_new, l, t_new):
    db, rows, _ = q.shape
    n_pages = page_table.shape[1]
    page = cache_t.shape[3]
    pc = _page_chunk(n_pages)
    per = page // NSA_BLOCK
    nch = n_pages // pc
    sel_c = sel.reshape(db, t_new, nch, pc * per).transpose(0, 2, 1, 3)
    expand = (jnp.arange(pc * per)[:, None] == jnp.arange(pc * page)[None, :] // NSA_BLOCK).astype(BF16)

    def page_spec(k):
        return pl.BlockSpec((None, None, 2 * NSA_HD, page), lambda bi, c, pt: (l, pt[bi, c * pc + k], 1, 0))

    grid_spec = pltpu.PrefetchScalarGridSpec(
        num_scalar_prefetch=1,
        grid=(db, nch),
        in_specs=[pl.BlockSpec((1, rows, NSA_HD), lambda bi, c, pt: (bi, 0, 0)),
                  pl.BlockSpec((1, 1, t_new, pc * per), lambda bi, c, pt: (bi, c, 0, 0)),
                  pl.BlockSpec((pc * per, pc * page), lambda bi, c, pt: (0, 0))]
        + [page_spec(k) for k in range(pc)]
        + [pl.BlockSpec((1, t_new, LANE), lambda bi, c, pt: (bi, 0, 0))],
        out_specs=pl.BlockSpec((1, rows, NSA_HD), lambda bi, c, pt: (bi, 0, 0)),
        scratch_shapes=[pltpu.VMEM((rows, 1), F32), pltpu.VMEM((rows, 1), F32), pltpu.VMEM((rows, NSA_HD), F32)],
    )
    return pl.pallas_call(
        functools.partial(_nsa_sel_sample_body, pc=pc, t_new=t_new),
        grid_spec=grid_spec,
        out_shape=jax.ShapeDtypeStruct((db, rows, NSA_HD), F32),
        compiler_params=_cparams(("parallel", "arbitrary")),
        name="nsa_sel_sample",
    )(page_table, q, sel_c, expand, *([cache_t] * pc), kv_new)


def _nsa_win_sample_body(q_ref, buf_ref, new_ref, ocmp_ref, osel_ref, g_ref, o_ref, *, t_new):
    heads = NSA_HEADS
    lw = buf_ref.shape[2]
    q = q_ref[0]
    buf_t = buf_ref[0]
    kv_new = new_ref[0]
    rel_b = (lax.broadcasted_iota(jnp.int32, (t_new, lw), 0) + lw
             - lax.broadcasted_iota(jnp.int32, (t_new, lw), 1))
    mask_b = (rel_b >= 0) & (rel_b < NSA_WINDOW)
    mask_n = (lax.broadcasted_iota(jnp.int32, (t_new, t_new), 1)
              <= lax.broadcasted_iota(jnp.int32, (t_new, t_new), 0))
    s_b = _masked(_dot(q, buf_t[:NSA_HD]), mask_b, heads, NEG)
    s_n = _masked(_dot_nt(q, kv_new[:, :NSA_HD]), mask_n, heads, NEG)
    m = jnp.maximum(jnp.max(s_b, -1, keepdims=True), jnp.max(s_n, -1, keepdims=True))
    e_b = _masked(jnp.exp(s_b - m), mask_b, heads, 0.0)
    e_n = _masked(jnp.exp(s_n - m), mask_n, heads, 0.0)
    den = jnp.sum(e_b, -1, keepdims=True) + jnp.sum(e_n, -1, keepdims=True)
    o_win = (_dot_nt(e_b, buf_t[NSA_HD:]) + _dot(e_n, kv_new[:, NSA_HD:])) / jnp.where(den > 0, den, 1.0)
    o_cmp = ocmp_ref[0]
    o_sel = osel_ref[0]
    g = g_ref[0]
    for h in range(heads):
        rows = slice(h * t_new, (h + 1) * t_new)
        o_ref[0, :, h * NSA_HD:(h + 1) * NSA_HD] = (
            g[:, h:h + 1] * o_cmp[rows] + g[:, heads + h:heads + h + 1] * o_sel[rows]
            + g[:, 2 * heads + h:2 * heads + h + 1] * o_win[rows])


def _nsa_win_sample(q, buf_t, kv_new, o_cmp, o_sel, gates, t_new):
    db, rows, _ = q.shape
    lw = buf_t.shape[2]
    blk = lambda r, w: pl.BlockSpec((1, r, w), lambda bi: (bi, 0, 0))
    return pl.pallas_call(
        functools.partial(_nsa_win_sample_body, t_new=t_new),
        grid=(db,),
        in_specs=[blk(rows, NSA_HD), blk(LANE, lw), blk(t_new, LANE), blk(rows, NSA_HD), blk(rows, NSA_HD),
                  blk(t_new, LANE)],
        out_specs=blk(t_new, BRANCH_W),
        out_shape=jax.ShapeDtypeStruct((db, t_new, BRANCH_W), F32),
        compiler_params=_cparams(("parallel",)),
        name="nsa_win_sample",
    )(q, buf_t, kv_new, o_cmp, o_sel, gates)


def _rwkv_pre_body(zr_ref, zp_ref, mu_ref, w0_ref, a0_ref, wa_ref, g2_ref, kk_ref, ka_ref, ones_ref,
                   r_ref, k_ref, v_ref, lw_ref, kkn_ref, b_ref, g_ref):
    c = BRANCH_W
    zr = zr_ref[...]
    zs = zr + (zp_ref[...] - zr) * mu_ref[...]
    r, k, v = zs[:, :c], zs[:, c:2 * c], zs[:, 2 * c:3 * c]
    lora = zs[:, 3 * c:3 * c + LANE]
    lane = lax.broadcasted_iota(jnp.int32, lora.shape, 1)
    wa = _dot(jnp.where(lane < RWKV_W_LORA, jnp.tanh(lora), lora), wa_ref[...])
    x = -(w0_ref[...] + wa[:, :c])
    softplus = jnp.maximum(x, 0.0) + jnp.log(1.0 + jnp.exp(-jnp.abs(x)))
    a = jax.nn.sigmoid(a0_ref[...] + wa[:, c:])
    kk = k * kk_ref[...]
    norm = jnp.sqrt(_dot_split(kk * kk, ones_ref[...]))
    kk = kk / jnp.maximum(norm, 1e-12)
    r_ref[...] = r
    k_ref[...] = k * (1.0 + (a - 1.0) * ka_ref[...])
    v_ref[...] = v
    lw_ref[...] = -jnp.exp(-softplus - 0.5)
    kkn_ref[...] = kk
    b_ref[...] = kk * a
    g_ref[...] = _dot(jax.nn.sigmoid(zs[:, 3 * c + LANE:]), g2_ref[...])


def _rwkv_pre(zr, zprev, mu, w0, a0, wa, g2, k_k, k_a, ones_bd, l):
    n = zr.shape[0]
    tm = _pick_tile(n, (256, 128))
    c = BRANCH_W
    row = lambda w: pl.BlockSpec((tm, w), lambda i: (i, 0))
    vec = lambda w: _vec_spec(w, l, 1)
    return pl.pallas_call(
        _rwkv_pre_body,
        grid=(n // tm,),
        in_specs=[row(RWKV_IN), row(RWKV_IN), vec(RWKV_IN), vec(c), vec(c),
                  pl.BlockSpec((None, LANE, 2 * c), lambda i: (l, 0, 0)),
                  pl.BlockSpec((None, RWKV_G_LORA, c), lambda i: (l, 0, 0)),
                  vec(c), vec(c), pl.BlockSpec((c, c), lambda i: (0, 0))],
        out_specs=[row(c)] * 7,
        out_shape=[jax.ShapeDtypeStruct((n, c), F32)] * 7,
        compiler_params=_cparams(("parallel",)),
        name="rwkv_pre",
    )(zr, zprev, _vec3(mu), _vec3(w0), _vec3(a0), wa, g2, _vec3(k_k), _vec3(k_a), ones_bd)


def _rwkv_scan_body(r_ref, k_ref, v_ref, lw_ref, kk_ref, b_ref, s0_ref, y_ref, sf_ref, st_ref, *, c_len, nh):
    ci = pl.program_id(1)

    @pl.when(ci == 0)
    def _():
        st_ref[...] = s0_ref[...]

    for si in range(st_ref.shape[0]):
        _rwkv_scan_seq(si, r_ref, k_ref, v_ref, lw_ref, kk_ref, b_ref, y_ref, st_ref, c_len, nh)

    @pl.when(ci == pl.num_programs(1) - 1)
    def _():
        sf_ref[...] = st_ref[...]


def _rwkv_scan_seq(si, r_ref, k_ref, v_ref, lw_ref, kk_ref, b_ref, y_ref, st_ref, c_len, nh):
    ch = nh * RWKV_HS
    rows = nh * c_len
    lw = lw_ref[si]
    tri = (lax.broadcasted_iota(jnp.int32, (c_len, c_len), 0)
           >= lax.broadcasted_iota(jnp.int32, (c_len, c_len), 1)).astype(BF16)
    lp = None
    rem = lw
    for _ in range(3):
        piece = rem.astype(BF16)
        d = jnp.dot(tri, piece, preferred_element_type=F32)
        lp = d if lp is None else lp + d
        rem = rem - piece.astype(F32)
    p = jnp.exp(lp)
    p_inv = jnp.exp(-lp)
    hm = (lax.broadcasted_iota(jnp.int32, (rows, ch), 0) // c_len
          == lax.broadcasted_iota(jnp.int32, (rows, ch), 1) // RWKV_HS)

    def stack(a):
        return jnp.where(hm, jnp.concatenate([a] * nh, 0), 0.0).astype(BF16)

    x_kk = stack(kk_ref[si] * jnp.exp(lp - lw))
    x_r = stack(r_ref[si] * p)
    y_k = stack(k_ref[si] * p_inv)
    y_b = stack(b_ref[si] * p_inv)
    v = v_ref[si]
    vs = jnp.concatenate([v[:, h * RWKV_HS:(h + 1) * RWKV_HS] for h in range(nh)], 0)

    ti = lax.broadcasted_iota(jnp.int32, (rows, rows), 0) % c_len
    tj = lax.broadcasted_iota(jnp.int32, (rows, rows), 1) % c_len
    strict, incl = ti > tj, ti >= tj
    n_mat = jnp.where(strict, -_dot_nt(x_kk, y_b), 0.0)
    a_kk = jnp.where(strict, _dot_nt(x_kk, y_k), 0.0)
    a_rk = jnp.where(incl, _dot_nt(x_r, y_k), 0.0)
    a_rb = jnp.where(incl, _dot_nt(x_r, y_b), 0.0)
    eye = jnp.where(lax.broadcasted_iota(jnp.int32, (rows, rows), 0)
                    == lax.broadcasted_iota(jnp.int32, (rows, rows), 1), 1.0, 0.0)
    t_inv = eye + n_mat
    pw = n_mat
    span = 2
    while span < c_len:
        pw = _dot(pw, pw)
        t_inv = t_inv + _dot(t_inv, pw)
        span *= 2

    st = st_ref[si]
    u = _dot(t_inv, _dot_nt(x_kk, st) + _dot(a_kk, vs))
    y = _dot_nt(x_r, st) + _dot(a_rk, vs) - _dot(a_rb, u)
    st_ref[si] = p[c_len - 1:c_len, :] * (st + _dot(vs.T, y_k) - _dot(u.T, y_b))
    for h in range(nh):
        y_ref[si, :, h * RWKV_HS:(h + 1) * RWKV_HS] = y[h * c_len:(h + 1) * c_len]


def _rwkv_scan(r, k, v, lw, kk, b, s0, c_len):
    nseq, t, ch = r.shape
    nh = ch // RWKV_HS
    assert nh * c_len == SCAN_ROWS
    ns = _pick_tile(nseq, (2, 1))
    blk = pl.BlockSpec((ns, c_len, ch), lambda si, ci: (si, ci, 0))
    st = pl.BlockSpec((ns, RWKV_HS, ch), lambda si, ci: (si, 0, 0))
    return pl.pallas_call(
        functools.partial(_rwkv_scan_body, c_len=c_len, nh=nh),
        grid=(nseq // ns, t // c_len),
        in_specs=[blk] * 6 + [st],
        out_specs=[blk, st],
        out_shape=[jax.ShapeDtypeStruct((nseq, t, ch), F32), jax.ShapeDtypeStruct((nseq, RWKV_HS, ch), F32)],
        scratch_shapes=[pltpu.VMEM((ns, RWKV_HS, ch), F32)],
        compiler_params=_cparams(("parallel", "arbitrary")),
        name="rwkv_scan",
    )(r, k, v, lw, kk, b, s0)


def _rwkv_post_body(y_ref, r_ref, k_ref, v_ref, g_ref, rk_ref, lnw_ref, lnb_ref, ones_ref, o_ref):
    y = y_ref[...]
    ones = ones_ref[...]
    mu = _dot_split(y, ones) * (1.0 / RWKV_HS)
    dev = y - mu
    var = _dot_split(dev * dev, ones) * (1.0 / RWKV_HS)
    yn = dev * lax.rsqrt(var + RWKV_GN_EPS) * lnw_ref[...] + lnb_ref[...]
    bonus = _dot_split(r_ref[...] * k_ref[...] * rk_ref[...], ones) * v_ref[...]
    o_ref[...] = (yn + bonus) * g_ref[...]


def _rwkv_post(y, r, k, v, g, r_k, ln_w, ln_b, ones_bd, l):
    n, c = y.shape
    tm = _pick_tile(n, (256, 128))
    row = pl.BlockSpec((tm, c), lambda i: (i, 0))
    vec = _vec_spec(c, l, 1)
    return pl.pallas_call(
        _rwkv_post_body,
        grid=(n // tm,),
        in_specs=[row] * 5 + [vec] * 3 + [pl.BlockSpec((c, c), lambda i: (0, 0))],
        out_specs=row,
        out_shape=jax.ShapeDtypeStruct((n, c), F32),
        compiler_params=_cparams(("parallel",)),
        name="rwkv_post",
    )(y, r, k, v, g, _vec3(r_k), _vec3(ln_w), _vec3(ln_b), ones_bd)


def _rope_tables(pos):
    inv = ROPE_THETA ** (-jnp.arange(0, NSA_HD, 2, dtype=F32) / NSA_HD)
    ang = pos.astype(F32)[:, None] * inv[None, :]
    c, s = jnp.cos(ang), jnp.sin(ang)
    return jnp.concatenate([c, c], -1), jnp.concatenate([-s, s], -1)


def _table(cos, sin, groups):
    n = cos.shape[0]
    cs = [cos if rot else jnp.ones((n, w), F32) for w, rot in groups]
    ss = [sin if rot else jnp.zeros((n, w), F32) for w, rot in groups]
    return jnp.concatenate(cs, -1), jnp.concatenate(ss, -1)


def _heads_first(x, tile, group, t_new):
    nt, h, _, d = x.shape
    x = x.reshape(nt, h, tile // t_new, t_new, d).transpose(0, 2, 1, 3, 4)
    return x.reshape(group, h * t_new, d)


def kernel(x_prompt, x_sample, cache_mla, cache_nsa, state_nsa_win, state_pool, state_rwkv, state_rwkv_shift,
           page_table, norm_ffn1, ffn1_w_gate, ffn1_w_up, ffn1_w_down, norm_mix, w_in, w_branch, w_out, pool_w,
           pool_scale, mla_q_norm, mla_w_uq, mla_kv_norm, mla_w_uk, mla_w_uv, nsa_phi_k, nsa_phi_v, rwkv_mu,
           rwkv_w0, rwkv_w2, rwkv_a0, rwkv_a2, rwkv_g2, rwkv_k_k, rwkv_k_a, rwkv_r_k, rwkv_ln_w, rwkv_ln_b,
           norm_ffn2, ffn2_w_gate, ffn2_w_up, ffn2_w_down, norm_final):
    b, s, d = x_prompt.shape
    db, t_new = x_sample.shape[:2]
    depth = norm_ffn1.shape[0]
    n_p, n_s = b * s, db * t_new
    n_pages, page = page_table.shape[1], cache_mla.shape[2]
    past = n_pages * page
    win_len = state_nsa_win.shape[2]
    n_phys = cache_mla.shape[1]
    assert d == D_MODEL and s % Q_TILE == 0 and n_s % Q_TILE == 0 and Q_TILE % t_new == 0
    assert s >= win_len and page % NSA_BLOCK == 0 and t_new < NSA_BLOCK and t_new <= POOL_HALO
    samp_c = t_new
    samp_grp = SCAN_ROWS // (samp_c * RWKV_HEADS)
    assert db % samp_grp == 0 and s % 32 == 0

    bf = lambda w: w.astype(BF16)
    tf = _pick_tile(ffn1_w_gate.shape[2], (512, 256, 128))
    f1 = (bf(_col_tiles(ffn1_w_gate, tf)), bf(_col_tiles(ffn1_w_up, tf)), bf(ffn1_w_down))
    f2 = (bf(_col_tiles(ffn2_w_gate, tf)), bf(_col_tiles(ffn2_w_up, tf)), bf(ffn2_w_down))
    o_mla = BRANCH_W
    o_nsa = o_mla + MLA_Q_LORA + MLA_CACHE_W
    o_rw = o_nsa + BRANCH_W + 6 * NSA_HD + 3 * NSA_HEADS
    o_gate = o_rw + RWKV_IN
    zpad = lambda w: jnp.zeros((depth, d, w), w_in.dtype)
    w_in_p = bf(_col_tiles(jnp.concatenate([
        w_in[:, :, o_gate:], w_in[:, :, :o_mla], w_in[:, :, o_mla:o_mla + MLA_Q_LORA],
        w_in[:, :, o_mla + MLA_Q_LORA:o_nsa], zpad(MLA_ROW_W - MLA_CACHE_W),
        w_in[:, :, o_nsa:o_nsa + BRANCH_W], w_in[:, :, o_nsa + BRANCH_W:o_nsa + BRANCH_W + 6 * NSA_HD],
        w_in[:, :, o_nsa + BRANCH_W + 6 * NSA_HD:o_rw], zpad(LANE - 3 * NSA_HEADS),
        w_in[:, :, o_rw:o_gate], zpad(Z_W - Z_END)], -1), Z_TILE))
    wb = bf(w_branch)
    wo = bf(w_out)
    pw = bf(pool_w)
    uq = mla_w_uq.reshape(depth, MLA_Q_LORA, MLA_HEADS, MLA_NOPE + MLA_ROPE)
    wuq = bf(jnp.concatenate([uq[..., :MLA_NOPE].reshape(depth, MLA_Q_LORA, -1),
                              uq[..., MLA_NOPE:].reshape(depth, MLA_Q_LORA, -1)], -1))
    wuk = bf(mla_w_uk.transpose(0, 2, 3, 1))
    wuv = bf(mla_w_uv.transpose(0, 2, 1, 3))
    zz = jnp.zeros((depth, NSA_HD, NSA_HD), F32)
    phi = bf(jnp.concatenate([jnp.concatenate([nsa_phi_k, zz], -1), jnp.concatenate([zz, nsa_phi_v], -1)], 1))
    phi_t = phi.transpose(0, 2, 1)
    zw =jnp.zeros((depth, RWKV_W_LORA, BRANCH_W), F32)
    wa = bf(jnp.concatenate([jnp.concatenate([rwkv_w2, zw], -1), jnp.concatenate([zw, rwkv_a2], -1)], 1))
    g2 = bf(rwkv_g2)
    r_k = rwkv_r_k.reshape(depth, BRANCH_W)
    ones_bd = (jnp.arange(BRANCH_W)[:, None] // RWKV_HS == jnp.arange(BRANCH_W)[None, :] // RWKV_HS).astype(BF16)

    pos = jnp.concatenate([jnp.tile(jnp.arange(s), b), jnp.tile(past + jnp.arange(t_new), db)])
    cos, sin = _rope_tables(pos)
    tab_mq = _table(cos, sin, [(MLA_HEADS * MLA_NOPE, False)] + [(MLA_ROPE, True)] * MLA_HEADS)
    tab_mkv = _table(cos, sin, [(MLA_KV_LORA, False), (MLA_ROPE, True), (MLA_ROW_W - MLA_CACHE_W, False)])
    tab_nq = _table(cos, sin, [(NSA_HD, True)] * NSA_HEADS)
    tab_nkv = _table(cos, sin, [(2 * NSA_HD, False), (NSA_HD, True), (NSA_HD, False), (NSA_HD, True),
                                (NSA_HD, False)])
    nb_p = s // NSA_BLOCK
    nb_s = past // NSA_BLOCK
    tab_bp = _table(*_rope_tables(jnp.arange(nb_p) * NSA_BLOCK + NSA_BLOCK - 1), [(NSA_HD, True), (NSA_HD, False)])
    tab_bs = [t.T for t in _rope_tables(jnp.arange(nb_s) * NSA_BLOCK + NSA_BLOCK - 1)]

    cache_mla_t = cache_mla.transpose(0, 1, 3, 2)
    cache_nsa_t = cache_nsa.transpose(0, 1, 3, 4, 2).reshape(depth, n_phys, 4 * NSA_HD, page)
    x = jnp.concatenate([x_prompt.reshape(n_p, d), x_sample.reshape(n_s, d)], 0)
    ns_tiles = n_s // Q_TILE
    new_p = [[] for _ in range(6)]
    new_s = [[] for _ in range(6)]
    for l in range(depth):
        x = _ffn(x, norm_ffn1[l], *f1, norm_final, l, False)
        z = _inproj(x, norm_mix[l], w_in_p, l)

        zp = z[:, Z_POOL:Z_POOL + BRANCH_W]
        zp_p = zp[:n_p].reshape(b, s, BRANCH_W)
        zp_s = zp[n_p:].reshape(db, t_new, BRANCH_W)
        pre_p = jnp.zeros((b, POOL_HALO, BRANCH_W), F32)
        pre_s = jnp.concatenate([jnp.zeros((db, POOL_HALO - POOL_BUF, BRANCH_W), F32), state_pool[l]], 1)
        o_pool_p = _pool(jnp.concatenate([pre_p, zp_p], 1), pw, pool_scale, l, 0, 1)
        o_pool_s = _pool(jnp.concatenate([pre_s, zp_s], 1), pw, pool_scale, l, past, _pick_tile(db, (32, 16, 8, 1)))
        o_pool = jnp.concatenate([o_pool_p.reshape(n_p, BRANCH_W), o_pool_s.reshape(n_s, BRANCH_W)], 0)
        new_p[3].append(zp_p[:, -POOL_BUF:])
        new_s[3].append(jnp.concatenate([state_pool[l], zp_s], 1)[:, -POOL_BUF:])

        qcat, qcat_t, rows, rows_b, rows_t = _mla_proj(
            z[:, Z_MQ:Z_MQ + MLA_Q_LORA], z[:, Z_MKV:Z_MKV + MLA_ROW_W], mla_q_norm, mla_kv_norm, wuq, wuk,
            tab_mq + tab_mkv, l)
        o_mla_p = _mla_prompt(qcat_t, rows_b[:n_p], rows_t, wuv, l, b, s)
        q_s = _heads_first(qcat[n_p // Q_TILE:], Q_TILE, db, t_new)
        o_mla_s = _mla_sample(page_table, q_s, cache_mla_t, rows_b[n_p:].reshape(db, t_new, MLA_ROW_W), wuv, l)
        o_mla_all = jnp.concatenate([o_mla_p, o_mla_s.reshape(n_s, BRANCH_W)], 0)
        new_p[0].append(rows[:n_p, :MLA_CACHE_W].reshape(b, s, MLA_CACHE_W))
        new_s[0].append(rows[n_p:, :MLA_CACHE_W].reshape(db, t_new, MLA_CACHE_W))

        nq, nq_t, nkv, ng = _nsa_proj(z[:, Z_NQ:Z_NQ + BRANCH_W], z[:, Z_NKV:Z_NKV + 6 * NSA_HD],
                                      z[:, Z_NG:Z_NG + LANE], tab_nq + tab_nkv)
        kbvb_p = _cmp_proj(_bmean_prompt(nkv[:n_p], b, s), phi, *tab_bp, l)
        o_nsa_p = _nsa_prompt(nq_t, kbvb_p, nkv[:n_p], ng[:n_p], b, s)
        nq_s = _heads_first(nq[n_p // Q_TILE:], Q_TILE, db, t_new)
        nkv_s = nkv[n_p:].reshape(db, t_new, 6 * NSA_HD)
        kbvb_s = _cmp_proj_t(_bmean_sample(page_table, cache_nsa_t, l), phi_t, *tab_bs, l)
        o_cmp_s, sel_s = _nsa_cmp_sample(nq_s, kbvb_s, t_new)
        o_sel_s = _nsa_sel_sample(page_table, nq_s, sel_s, cache_nsa_t, nkv_s[:, :, 2 * NSA_HD:4 * NSA_HD], l, t_new)
        win_buf_t = state_nsa_win[l].transpose(0, 2, 3, 1).reshape(db, 2 * NSA_HD, win_len)
        o_nsa_s = _nsa_win_sample(nq_s, win_buf_t, nkv_s[:, :, 4 * NSA_HD:], o_cmp_s, o_sel_s,
                                  ng[n_p:].reshape(db, t_new, LANE), t_new)
        o_nsa_all = jnp.concatenate([o_nsa_p, o_nsa_s.reshape(n_s, BRANCH_W)], 0)
        nkv_p = nkv[:n_p].reshape(b, s, 6, NSA_HD)
        new_p[1].append(nkv_p[:, :, :4])
        new_s[1].append(nkv_s[:, :, :4 * NSA_HD].reshape(db, t_new, 4, NSA_HD))
        new_p[2].append(nkv_p[:, s - win_len:, 4:])
        new_s[2].append(jnp.concatenate([state_nsa_win[l], nkv_s[:, :, 4 * NSA_HD:].reshape(db, t_new, 2, NSA_HD)],
                                        1)[:, -win_len:])

        zr = z[:, Z_RW:Z_RW + RWKV_IN]
        zr_p = zr[:n_p].reshape(b, s, RWKV_IN)
        zr_s = zr[n_p:].reshape(db, t_new, RWKV_IN)
        zprev = jnp.concatenate([
            jnp.concatenate([jnp.zeros((b, 1, RWKV_IN), F32), zr_p[:, :-1]], 1).reshape(n_p, RWKV_IN),
            jnp.concatenate([state_rwkv_shift[l][:, None], zr_s[:, :-1]], 1).reshape(n_s, RWKV_IN)], 0)
        pre = _rwkv_pre(zr, zprev, rwkv_mu, rwkv_w0, rwkv_a0, wa, g2, rwkv_k_k, rwkv_k_a, ones_bd, l)
        r_, k_, v_, lw_, kk_, b_, g_ = pre
        seq_p = lambda a: a[:n_p].reshape(b, s, BRANCH_W)
        seq_s = lambda a: (a[n_p:].reshape(db // samp_grp, samp_grp, t_new, BRANCH_W).transpose(0, 2, 1, 3)
                           .reshape(db // samp_grp, t_new, samp_grp * BRANCH_W))
        scan_in = (r_, k_, v_, lw_, kk_, b_)
        y_p, sf_p = _rwkv_scan(*[seq_p(a) for a in scan_in],
                               jnp.zeros((b, RWKV_HS, BRANCH_W), F32), SCAN_ROWS // RWKV_HEADS)
        s0_s = (state_rwkv[l].reshape(db // samp_grp, samp_grp, RWKV_HEADS, RWKV_HS, RWKV_HS)
                .transpose(0, 3, 1, 2, 4).reshape(db // samp_grp, RWKV_HS, samp_grp * BRANCH_W))
        y_s, sf_s = _rwkv_scan(*[seq_s(a) for a in scan_in], s0_s, samp_c)
        y_s = (y_s.reshape(db // samp_grp, t_new, samp_grp, BRANCH_W).transpose(0, 2, 1, 3)
               .reshape(n_s, BRANCH_W))
        y_all = jnp.concatenate([y_p.reshape(n_p, BRANCH_W), y_s], 0)
        o_rwkv = _rwkv_post(y_all, r_, k_, v_, g_, r_k, rwkv_ln_w, rwkv_ln_b, ones_bd, l)
        new_p[4].append(sf_p.reshape(b, RWKV_HS, RWKV_HEADS, RWKV_HS).transpose(0, 2, 1, 3))
        new_s[4].append(sf_s.reshape(db // samp_grp, RWKV_HS, samp_grp, RWKV_HEADS, RWKV_HS)
                        .transpose(0, 2, 3, 1, 4).reshape(db, RWKV_HEADS, RWKV_HS, RWKV_HS))
        new_p[5].append(zr_p[:, -1])
        new_s[5].append(zr_s[:, -1])

        branches = jnp.concatenate([o_pool, o_mla_all, o_nsa_all, o_rwkv], -1)
        x = _merge(x, branches, z, wb, wo, l)
        x = _ffn(x, norm_ffn2[l], *f2, norm_final, l, l == depth - 1)

    y_prompt = x[:n_p].reshape(b, s, d)
    y_sample = x[n_p:].reshape(db, t_new, d)
    outs_p = [jnp.stack(a, 0) for a in new_p]
    outs_s = [jnp.stack(a, 0) for a in new_s]
    res = [y_prompt, y_sample]
    for a, c in zip(outs_p, outs_s):
        res += [a, c]
    return tuple(res)
```

```python
import functools

import jax
import jax.numpy as jnp
from jax import lax
from jax.experimental import pallas as pl
from jax.experimental.pallas import tpu as pltpu

F32 = jnp.float32
BF16 = jnp.bfloat16

D_MODEL = 2048
N_BRANCH = 4
BRANCH_W = D_MODEL // N_BRANCH
ROPE_THETA = 10000.0
RMS_EPS = 1e-6
NEG = -1e30
MASK_BIAS = -2e30
POOL_WINDOWS = (2, 4, 8, 16)
POOL_GW = BRANCH_W // len(POOL_WINDOWS)
POOL_BUF = max(POOL_WINDOWS) - 1
POOL_HALO = 16
MLA_HEADS = 4
MLA_NOPE = 128
MLA_ROPE = 64
MLA_V = BRANCH_W // MLA_HEADS
MLA_Q_LORA = D_MODEL // 4
MLA_KV_LORA = D_MODEL // 8
MLA_CACHE_W = MLA_KV_LORA + MLA_ROPE
MLA_ROW_W = 384
MLA_SCALE = (MLA_NOPE + MLA_ROPE) ** -0.5
NSA_HEADS = 8
NSA_HD = BRANCH_W // NSA_HEADS
NSA_BLOCK = 64
NSA_TOPN = 16
NSA_WINDOW = 512
NSA_SCALE = NSA_HD ** -0.5
RWKV_HS = 64
RWKV_HEADS = BRANCH_W // RWKV_HS
RWKV_W_LORA = 64
RWKV_A_LORA = 64
RWKV_G_LORA = 128
RWKV_GN_EPS = 64e-5
RWKV_IN = 3 * BRANCH_W + RWKV_W_LORA + RWKV_A_LORA + RWKV_G_LORA
SCAN_ROWS = 256
Q_TILE = 128
FFN_ROW_PARTS = 2
PAGE_GROUP = 64
LANE = 128

Z_GATE = 0
Z_POOL = N_BRANCH * D_MODEL
Z_MQ = Z_POOL + BRANCH_W
Z_MKV = Z_MQ + MLA_Q_LORA
Z_NQ = Z_MKV + MLA_ROW_W
Z_NKV = Z_NQ + BRANCH_W
Z_NG = Z_NKV + 6 * NSA_HD
Z_RW = Z_NG + LANE
Z_END = Z_RW + RWKV_IN
Z_TILE = 512
Z_W = -(-Z_END // Z_TILE) * Z_TILE


def _cparams(sem, vmem_mb=48):
    return pltpu.CompilerParams(dimension_semantics=sem, vmem_limit_bytes=vmem_mb * 1024 * 1024)


def _pick_tile(n, cands):
    for c in cands:
        if n % c == 0:
            return c
    raise ValueError(f"no tile in {cands} divides {n}")


def _vec3(a):
    return a.reshape(a.shape[0], 1, a.shape[1])


def _vec_spec(w, l, ngrid):
    if ngrid == 1:
        return pl.BlockSpec((None, 1, w), lambda i: (l, 0, 0))
    return pl.BlockSpec((None, 1, w), lambda i, j: (l, 0, 0))


def _dot(a, b):
    return jnp.dot(a.astype(BF16), b.astype(BF16), preferred_element_type=F32)


def _dot_nt(a, b):
    return lax.dot_general(a.astype(BF16), b.astype(BF16), (((1,), (1,)), ((), ())),
                           preferred_element_type=F32)


def _dot_split(a, b01, terms=2):
    acc = None
    rem = a
    for _ in range(terms):
        piece = rem.astype(BF16)
        d = jnp.dot(piece, b01, preferred_element_type=F32)
        acc = d if acc is None else acc + d
        rem = rem - piece.astype(F32)
    return acc


def _rms(x, g):
    return x * lax.rsqrt(jnp.mean(x * x, -1, keepdims=True) + RMS_EPS) * g


def _rope_apply(x, cos, sin):
    w = x.shape[-1]
    lane = lax.broadcasted_iota(jnp.int32, x.shape, x.ndim - 1)
    fwd = pltpu.roll(x, w - 32, x.ndim - 1)
    bwd = pltpu.roll(x, 32, x.ndim - 1)
    return x * cos + jnp.where((lane & 63) < 32, fwd, bwd) * sin


def _ffn_body(x_ref, g_ref, wg_ref, wu_ref, wd_ref, gf_ref, o_ref, h_ref, acc_ref, *, nj, final_norm):
    j = pl.program_id(1)

    @pl.when(j == 0)
    def _():
        h_ref[...] = _rms(x_ref[...], g_ref[...]).astype(BF16)
        acc_ref[...] = jnp.zeros(acc_ref.shape, F32)

    part = x_ref.shape[0] // FFN_ROW_PARTS
    for r in [pl.ds(k * part, part) for k in range(FFN_ROW_PARTS)]:
        h = h_ref[r]
        gt = jnp.dot(h, wg_ref[...], preferred_element_type=F32)
        up = jnp.dot(h, wu_ref[...], preferred_element_type=F32)
        acc_ref[r] += _dot(gt * jax.nn.sigmoid(gt) * up, wd_ref[...])

    @pl.when(j == nj - 1)
    def _():
        y = x_ref[...] + 0.5 * acc_ref[...]
        if final_norm:
            y = _rms(y, gf_ref[...])
        o_ref[...] = y


def _col_tiles(w, t):
    layers, d, f = w.shape
    return w.reshape(layers, d, f // t, t).transpose(0, 2, 1, 3)


def _ffn(x, g, wg, wu, wd, gf, l, final_norm):
    n, d = x.shape
    nj, tf = wg.shape[1], wg.shape[3]
    tm = _pick_tile(n, (512, 256, 128))
    return pl.pallas_call(
        functools.partial(_ffn_body, nj=nj, final_norm=final_norm),
        grid=(n // tm, nj),
        in_specs=[
            pl.BlockSpec((tm, d), lambda i, j: (i, 0)),
            pl.BlockSpec((1, d), lambda i, j: (0, 0)),
            pl.BlockSpec((None, None, d, tf), lambda i, j: (l, j, 0, 0)),
            pl.BlockSpec((None, None, d, tf), lambda i, j: (l, j, 0, 0)),
            pl.BlockSpec((None, tf, d), lambda i, j: (l, j, 0)),
            pl.BlockSpec((1, d), lambda i, j: (0, 0)),
        ],
        out_specs=pl.BlockSpec((tm, d), lambda i, j: (i, 0)),
        out_shape=jax.ShapeDtypeStruct((n, d), F32),
        scratch_shapes=[pltpu.VMEM((tm, d), BF16), pltpu.VMEM((tm, d), F32)],
        compiler_params=_cparams(("parallel", "arbitrary")),
        name="ffn",
    )(x, g.reshape(1, d), wg, wu, wd, gf.reshape(1, d))


def _inproj_body(x_ref, g_ref, w_ref, o_ref, h_ref):
    @pl.when(pl.program_id(1) == 0)
    def _():
        h_ref[...] = _rms(x_ref[...], g_ref[...]).astype(BF16)

    o_ref[...] = jnp.dot(h_ref[...], w_ref[...], preferred_element_type=F32)


def _inproj(x, g, w, l):
    n, d = x.shape
    zw = w.shape[1] * Z_TILE
    tm = _pick_tile(n, (1024, 512, 256, 128))
    return pl.pallas_call(
        _inproj_body,
        grid=(n // tm, zw // Z_TILE),
        in_specs=[
            pl.BlockSpec((tm, d), lambda i, j: (i, 0)),
            pl.BlockSpec((1, d), lambda i, j: (0, 0)),
            pl.BlockSpec((None, None, d, Z_TILE), lambda i, j: (l, j, 0, 0)),
        ],
        out_specs=pl.BlockSpec((tm, Z_TILE), lambda i, j: (i, j)),
        out_shape=jax.ShapeDtypeStruct((n, zw), F32),
        scratch_shapes=[pltpu.VMEM((tm, d), BF16)],
        compiler_params=_cparams(("parallel", "arbitrary")),
        name="inproj",
    )(x, g.reshape(1, d), w)


def _merge_body(x_ref, b0_ref, b1_ref, b2_ref, b3_ref, zg_ref, wb_ref, wo_ref, o_ref, acc_ref):
    n = pl.program_id(1)
    br = jnp.where(n == 0, b0_ref[...], jnp.where(n == 1, b1_ref[...], jnp.where(n == 2, b2_ref[...], b3_ref[...])))
    t = jax.nn.sigmoid(zg_ref[...]) * _dot(br, wb_ref[...])

    @pl.when(n == 0)
    def _():
        acc_ref[...] = t

    @pl.when(n > 0)
    def _():
        acc_ref[...] += t

    @pl.when(n == N_BRANCH - 1)
    def _():
        o_ref[...] = x_ref[...] + _dot(acc_ref[...], wo_ref[...])


def _merge(x, branches, z, wb, wo, l):
    n, d = x.shape
    tm = _pick_tile(n, (256, 128))
    return pl.pallas_call(
        _merge_body,
        grid=(n // tm, N_BRANCH),
        in_specs=[
            pl.BlockSpec((tm, d), lambda i, k: (i, 0)),
        ] + [pl.BlockSpec((tm, BRANCH_W), lambda i, k: (i, 0))] * N_BRANCH + [
            pl.BlockSpec((tm, d), lambda i, k: (i, k)),
            pl.BlockSpec((None, None, BRANCH_W, d), lambda i, k: (l, k, 0, 0)),
            pl.BlockSpec((None, d, d), lambda i, k: (l, 0, 0)),
        ],
        out_specs=pl.BlockSpec((tm, d), lambda i, k: (i, 0)),
        out_shape=jax.ShapeDtypeStruct((n, d), F32),
        scratch_shapes=[pltpu.VMEM((tm, d), F32)],
        compiler_params=_cparams(("parallel", "arbitrary")),
        name="merge",
    )(x, *branches, z, wb, wo)


def _pool_body(u_ref, w_ref, sc_ref, o_ref, *, t_len, t_chunk, pos0):
    g = pl.program_id(1)
    bt = u_ref.shape[0]
    for gi, win in enumerate(POOL_WINDOWS):
        @pl.when(g == gi)
        def _(win=win):
            for c0 in range(0, t_len, t_chunk):
                ue = u_ref[:, c0:c0 + t_chunk + POOL_HALO, :]
                acc = ue
                span = 1
                while span < win:
                    acc = acc[:, span:, :] + acc[:, :-span, :]
                    span *= 2
                wsum = acc[:, POOL_HALO + 1 - win:POOL_HALO + 1 - win + t_chunk, :]
                pos = pos0 + c0 + lax.broadcasted_iota(jnp.int32, (1, t_chunk, 1), 1)
                cnt = jnp.minimum(win, pos + 1).astype(F32)
                diff = wsum / cnt - ue[:, POOL_HALO:, :]
                y = _dot(diff.reshape(bt * t_chunk, LANE), w_ref[...]) * sc_ref[...]
                o_ref[:, c0:c0 + t_chunk, :] = y.reshape(bt, t_chunk, LANE)


def _pool(u_ext, w, scale, l, pos0, bt):
    b, te, c = u_ext.shape
    t_len = te - POOL_HALO
    t_chunk = _pick_tile(t_len, (512, 256, 128, 8))
    return pl.pallas_call(
        functools.partial(_pool_body, t_len=t_len, t_chunk=t_chunk, pos0=pos0),
        grid=(b // bt, len(POOL_WINDOWS)),
        in_specs=[
            pl.BlockSpec((bt, te, LANE), lambda i, g: (i, 0, g)),
            pl.BlockSpec((None, None, POOL_GW, POOL_GW), lambda i, g: (l, g, 0, 0)),
            pl.BlockSpec((None, 1, LANE), lambda i, g: (l, 0, g)),
        ],
        out_specs=pl.BlockSpec((bt, t_len, LANE), lambda i, g: (i, 0, g)),
        out_shape=jax.ShapeDtypeStruct((b, t_len, c), F32),
        compiler_params=_cparams(("parallel", "arbitrary")),
        name="pool",
    )(u_ext, w, _vec3(scale))


def _mla_proj_body(zq_ref, zkv_ref, qn_ref, kvn_ref, wuq_ref, wuk_ref, cq_ref, sq_ref, ckv_ref, skv_ref,
                   qcat_ref, qcat_t_ref, rows_ref, rowsb_ref, rows_t_ref):
    tm = zq_ref.shape[0]
    zk = zkv_ref[...]
    ckv = _rms(zk[:, :MLA_KV_LORA], kvn_ref[...])
    roped = _rope_apply(zk, ckv_ref[...], skv_ref[...])
    rows = jnp.concatenate([ckv, roped[:, MLA_KV_LORA:]], -1)
    rows_ref[...] = rows
    rowsb_ref[...] = rows.astype(BF16)
    rows_t_ref[0] = rows.T.astype(BF16)
    q = _dot(_rms(zq_ref[...], qn_ref[...]), wuq_ref[...])
    q = _rope_apply(q, cq_ref[...], sq_ref[...])
    pad = jnp.zeros((tm, MLA_ROW_W - MLA_CACHE_W), F32)
    for h in range(MLA_HEADS):
        q_lat = _dot(q[:, h * MLA_NOPE:(h + 1) * MLA_NOPE], wuk_ref[h])
        q_pe = q[:, MLA_HEADS * MLA_NOPE + h * MLA_ROPE:MLA_HEADS * MLA_NOPE + (h + 1) * MLA_ROPE]
        qcat = jnp.concatenate([q_lat, q_pe, pad], -1) * MLA_SCALE
        qcat_ref[0, h] = qcat.astype(BF16)
        qcat_t_ref[0, :, h * tm:(h + 1) * tm] = qcat.T.astype(BF16)


def _mla_proj(zq, zkv, qn, kvn, wuq, wuk, tabs, l):
    n = zq.shape[0]
    tm = Q_TILE
    qw = wuq.shape[2]
    row = lambda w: pl.BlockSpec((tm, w), lambda i: (i, 0))
    return pl.pallas_call(
        _mla_proj_body,
        grid=(n // tm,),
        in_specs=[
            row(MLA_Q_LORA), row(MLA_ROW_W),
            _vec_spec(MLA_Q_LORA, l, 1), _vec_spec(MLA_KV_LORA, l, 1),
            pl.BlockSpec((None, MLA_Q_LORA, qw), lambda i: (l, 0, 0)),
            pl.BlockSpec((None, MLA_HEADS, MLA_NOPE, MLA_KV_LORA), lambda i: (l, 0, 0, 0)),
            row(qw), row(qw), row(MLA_ROW_W), row(MLA_ROW_W),
        ],
        out_specs=[
            pl.BlockSpec((1, MLA_HEADS, tm, MLA_ROW_W), lambda i: (i, 0, 0, 0)),
            pl.BlockSpec((1, MLA_ROW_W, MLA_HEADS * tm), lambda i: (i, 0, 0)),
            row(MLA_ROW_W), row(MLA_ROW_W),
            pl.BlockSpec((1, MLA_ROW_W, tm), lambda i: (i, 0, 0)),
        ],
        out_shape=[
            jax.ShapeDtypeStruct((n // tm, MLA_HEADS, tm, MLA_ROW_W), BF16),
            jax.ShapeDtypeStruct((n // tm, MLA_ROW_W, MLA_HEADS * tm), BF16),
            jax.ShapeDtypeStruct((n, MLA_ROW_W), F32),
            jax.ShapeDtypeStruct((n, MLA_ROW_W), BF16),
            jax.ShapeDtypeStruct((n // tm, MLA_ROW_W, tm), BF16),
        ],
        compiler_params=_cparams(("parallel",)),
        name="mla_proj",
    )(zq, zkv, _vec3(qn), _vec3(kvn), wuq, wuk, *tabs)


def _masked(x, mask, heads, fill):
    if mask is None:
        return x
    tq, tk = mask.shape
    return jnp.where(mask[None], x.reshape(heads, tq, tk), fill).reshape(heads * tq, tk)


def _softmax_update(state, s, mask, heads, v, vt=False):
    m_old, l_old, acc_old = state
    s = _masked(s, mask, heads, NEG)
    m_new = jnp.maximum(m_old, jnp.max(s, -1, keepdims=True))
    alpha = jnp.exp(m_old - m_new)
    p = _masked(jnp.exp(s - m_new), mask, heads, 0.0)
    return (m_new, alpha * l_old + jnp.sum(p, -1, keepdims=True),
            alpha * acc_old + (_dot_nt(p, v) if vt else _dot(p, v)))


def _softmax_step(s, mask, heads, m_ref, l_ref, acc_ref, v, vt=False):
    m_ref[...], l_ref[...], acc_ref[...] = _softmax_update((m_ref[...], l_ref[...], acc_ref[...]), s, mask, heads,
                                                           v, vt)


def _softmax_once(s, mask, heads, v, vt=False):
    s = _masked(s, mask, heads, NEG)
    e = _masked(jnp.exp(s - jnp.max(s, -1, keepdims=True)), mask, heads, 0.0)
    den = jnp.sum(e, -1, keepdims=True)
    p = e / jnp.where(den > 0, den, 1.0)
    return (_dot_nt(p, v) if vt else _dot(p, v)), p


def _lanes(a, heads):
    return jnp.concatenate([a] * heads, 1)


def _softmax_step_t(st, bias, m_ref, l_ref, acc_ref, vt):
    if bias is not None:
        st = st + bias
    m_old = m_ref[...]
    m_new = jnp.maximum(m_old, jnp.max(st, 0, keepdims=True))
    alpha = jnp.exp(m_old - m_new)
    p = jnp.exp(st - m_new)
    l_ref[...] = alpha * l_ref[...] + jnp.sum(p, 0, keepdims=True)
    acc_ref[...] = alpha * acc_ref[...] + _dot(vt, p)
    m_ref[...] = m_new


def _softmax_once_t(st, keep, vt):
    keep = keep > 0.5
    st = jnp.where(keep, st, NEG)
    e = jnp.where(keep, jnp.exp(st - jnp.max(st, 0, keepdims=True)), 0.0)
    den = jnp.sum(e, 0, keepdims=True)
    p = e / jnp.where(den > 0, den, 1.0)
    return _dot(vt, p), p


def _softmax_init(m_ref, l_ref, acc_ref):
    m_ref[...] = jnp.full(m_ref.shape, NEG, F32)
    l_ref[...] = jnp.zeros(l_ref.shape, F32)
    acc_ref[...] = jnp.zeros(acc_ref.shape, F32)


def _softmax_out(l_ref, acc_ref):
    l = l_ref[...]
    return acc_ref[...] / jnp.where(l > 0, l, 1.0)


def _mla_prompt_body(qt_ref, k_ref, kt_ref, wuv_ref, o_ref, m_ref, l_ref, acc_ref):
    i = pl.program_id(1)
    tq = kt_ref.shape[2]
    qt = qt_ref[0]
    _softmax_init(m_ref, l_ref, acc_ref)

    def tile(j):
        return k_ref[0, pl.ds(pl.multiple_of(j * tq, tq), tq), :], kt_ref[j, :MLA_KV_LORA, :]

    def pair_step(jj, carry):
        k = k_ref[0, pl.ds(pl.multiple_of(jj * 2 * tq, 2 * tq), 2 * tq), :]
        vt = jnp.concatenate([kt_ref[2 * jj, :MLA_KV_LORA, :], kt_ref[2 * jj + 1, :MLA_KV_LORA, :]], 1)
        _softmax_step_t(_dot(k, qt), None, m_ref, l_ref, acc_ref, vt)
        return carry

    lax.fori_loop(0, i // 2, pair_step, 0)

    @pl.when(i % 2 == 1)
    def _():
        k, vt = tile(i - 1)
        _softmax_step_t(_dot(k, qt), None, m_ref, l_ref, acc_ref, vt)

    k, vt = tile(i)
    causal = (lax.broadcasted_iota(jnp.int32, (tq, tq), 0) <= lax.broadcasted_iota(jnp.int32, (tq, tq), 1))
    _softmax_step_t(_dot(k, qt), _lanes(jnp.where(causal, 0.0, MASK_BIAS), MLA_HEADS), m_ref, l_ref, acc_ref, vt)
    o_lat_t = acc_ref[...] / l_ref[...]
    for h in range(MLA_HEADS):
        o_ref[:, h * MLA_V:(h + 1) * MLA_V] = _dot(o_lat_t[:, h * tq:(h + 1) * tq].T, wuv_ref[h])


def _mla_prompt(qcat_t, rows_b, rows_t, wuv, l, b, s):
    tq = Q_TILE
    nq = s // tq
    rows = MLA_HEADS * tq
    return pl.pallas_call(
        _mla_prompt_body,
        grid=(b, nq),
        in_specs=[
            pl.BlockSpec((1, MLA_ROW_W, rows), lambda bi, i: (bi * nq + i, 0, 0)),
            pl.BlockSpec((1, s, MLA_ROW_W), lambda bi, i: (bi, 0, 0)),
            pl.BlockSpec((nq, MLA_ROW_W, tq), lambda bi, i: (bi, 0, 0)),
            pl.BlockSpec((None, MLA_HEADS, MLA_KV_LORA, MLA_V), lambda bi, i: (l, 0, 0, 0)),
        ],
        out_specs=pl.BlockSpec((tq, BRANCH_W), lambda bi, i: (bi * nq + i, 0)),
        out_shape=jax.ShapeDtypeStruct((b * s, BRANCH_W), F32),
        scratch_shapes=[pltpu.VMEM((1, rows), F32), pltpu.VMEM((1, rows), F32),
                        pltpu.VMEM((MLA_KV_LORA, rows), F32)],
        compiler_params=_cparams(("parallel", "arbitrary")),
        name="mla_prompt",
    )(qcat_t, rows_b.reshape(b, s, MLA_ROW_W), rows_t, wuv)


def _mla_sample_body(pt_ref, q_ref, *refs, pc, t_new):
    page_refs = refs[:pc]
    new_ref, wuv_ref, o_ref, m_ref, l_ref, acc_ref = refs[pc:]
    c = pl.program_id(1)

    @pl.when(c == 0)
    def _():
        _softmax_init(m_ref, l_ref, acc_ref)

    q = q_ref[0][:, :MLA_CACHE_W]
    state = (m_ref[...], l_ref[...], acc_ref[...])
    for g in range(0, pc, PAGE_GROUP):
        keys_t = jnp.concatenate([r[...].astype(BF16) for r in page_refs[g:g + PAGE_GROUP]], 1)
        state = _softmax_update(state, _dot(q, keys_t), None, MLA_HEADS, keys_t[:MLA_KV_LORA], vt=True)
    m_ref[...], l_ref[...], acc_ref[...] = state

    @pl.when(c == pl.num_programs(1) - 1)
    def _():
        k_new = new_ref[0][:, :MLA_CACHE_W]
        tq = lax.broadcasted_iota(jnp.int32, (t_new, t_new), 0)
        tk = lax.broadcasted_iota(jnp.int32, (t_new, t_new), 1)
        _softmax_step(_dot_nt(q, k_new), tk <= tq, MLA_HEADS, m_ref, l_ref, acc_ref, k_new[:, :MLA_KV_LORA])
        o_lat = _softmax_out(l_ref, acc_ref)
        for h in range(MLA_HEADS):
            o_ref[0, :, h * MLA_V:(h + 1) * MLA_V] = _dot(o_lat[h * t_new:(h + 1) * t_new], wuv_ref[h])


def _page_chunk(n_pages):
    return _pick_tile(n_pages, (64, 32, 16, 8, 4, 2, 1))


def _mla_sample(page_table, q, cache_t, rows_new, wuv, l):
    db, rows, _ = q.shape
    t_new = rows // MLA_HEADS
    n_pages = page_table.shape[1]
    page = cache_t.shape[3]
    pc = _page_chunk(n_pages)

    def page_spec(k):
        return pl.BlockSpec((None, None, MLA_CACHE_W, page),
                            lambda bi, c, pt: (l, pt[bi, c * pc + k], 0, 0))

    grid_spec = pltpu.PrefetchScalarGridSpec(
        num_scalar_prefetch=1,
        grid=(db, n_pages // pc),
        in_specs=[pl.BlockSpec((1, rows, MLA_ROW_W), lambda bi, c, pt: (bi, 0, 0))]
        + [page_spec(k) for k in range(pc)]
        + [pl.BlockSpec((1, t_new, MLA_ROW_W), lambda bi, c, pt: (bi, 0, 0)),
           pl.BlockSpec((None, MLA_HEADS, MLA_KV_LORA, MLA_V), lambda bi, c, pt: (l, 0, 0, 0))],
        out_specs=pl.BlockSpec((1, t_new, BRANCH_W), lambda bi, c, pt: (bi, 0, 0)),
        scratch_shapes=[pltpu.VMEM((rows, 1), F32), pltpu.VMEM((rows, 1), F32),
                        pltpu.VMEM((rows, MLA_KV_LORA), F32)],
    )
    return pl.pallas_call(
        functools.partial(_mla_sample_body, pc=pc, t_new=t_new),
        grid_spec=grid_spec,
        out_shape=jax.ShapeDtypeStruct((db, t_new, BRANCH_W), F32),
        compiler_params=_cparams(("parallel", "arbitrary")),
        name="mla_sample",
    )(page_table, q, *([cache_t] * pc), rows_new, wuv)


def _nsa_proj_body(zq_ref, zkv_ref, zg_ref, cq_ref, sq_ref, ckv_ref, skv_ref, q_ref, qt_ref, kv_ref, g_ref):
    tm = zq_ref.shape[0]
    q = _rope_apply(zq_ref[...], cq_ref[...], sq_ref[...]) * NSA_SCALE
    q_t = q.T
    for h in range(NSA_HEADS):
        q_ref[0, h] = q[:, h * NSA_HD:(h + 1) * NSA_HD].astype(BF16)
        qt_ref[0, :, h * tm:(h + 1) * tm] = q_t[h * NSA_HD:(h + 1) * NSA_HD].astype(BF16)
    kv_ref[...] = _rope_apply(zkv_ref[...], ckv_ref[...], skv_ref[...])
    g_ref[...] = jax.nn.sigmoid(zg_ref[...])


def _nsa_proj(zq, zkv, zg, tabs):
    n = zq.shape[0]
    tm = Q_TILE
    kvw = zkv.shape[1]
    row = lambda w: pl.BlockSpec((tm, w), lambda i: (i, 0))
    return pl.pallas_call(
        _nsa_proj_body,
        grid=(n // tm,),
        in_specs=[row(BRANCH_W), row(kvw), row(LANE), row(BRANCH_W), row(BRANCH_W), row(kvw), row(kvw)],
        out_specs=[pl.BlockSpec((1, NSA_HEADS, tm, NSA_HD), lambda i: (i, 0, 0, 0)),
                   pl.BlockSpec((1, NSA_HD, NSA_HEADS * tm), lambda i: (i, 0, 0)), row(kvw), row(LANE)],
        out_shape=[
            jax.ShapeDtypeStruct((n // tm, NSA_HEADS, tm, NSA_HD), BF16),
            jax.ShapeDtypeStruct((n // tm, NSA_HD, NSA_HEADS * tm), BF16),
            jax.ShapeDtypeStruct((n, kvw), F32),
            jax.ShapeDtypeStruct((n, LANE), F32),
        ],
        compiler_params=_cparams(("parallel",)),
        name="nsa_proj",
    )(zq, zkv, zg, *tabs)


def _bmean_prompt_body(kv_ref, o_ref):
    nb = o_ref.shape[1]
    x = kv_ref[0][:nb * NSA_BLOCK]
    o_ref[0] = jnp.sum(x.reshape(nb, NSA_BLOCK, LANE), 1) * (1.0 / NSA_BLOCK)


def _bmean_prompt(kv, b, s):
    nb = s // NSA_BLOCK
    return pl.pallas_call(
        _bmean_prompt_body,
        grid=(b,),
        in_specs=[pl.BlockSpec((1, s, LANE), lambda bi: (bi, 0, 0))],
        out_specs=pl.BlockSpec((1, nb, LANE), lambda bi: (bi, 0, 0)),
        out_shape=jax.ShapeDtypeStruct((b, nb, LANE), F32),
        compiler_params=_cparams(("parallel",)),
        name="bmean_prompt",
    )(kv.reshape(b, s, kv.shape[-1]))


def _bmean_sample_body(pt_ref, *refs, pc):
    a_ref, o_ref = refs[pc], refs[pc + 1]
    x = jnp.concatenate([r[...].astype(BF16) for r in refs[:pc]], 1)
    o_ref[0, 0] = jnp.dot(x, a_ref[...], preferred_element_type=F32) * (1.0 / NSA_BLOCK)


def _bmean_sample(page_table, cache_t, l):
    db, n_pages = page_table.shape
    page = cache_t.shape[3]
    pc = _page_chunk(n_pages)
    per = page // NSA_BLOCK
    nch = n_pages // pc
    ind = (jnp.arange(pc * page)[:, None] // NSA_BLOCK == jnp.arange(pc * per)[None, :]).astype(BF16)

    def page_spec(k):
        return pl.BlockSpec((None, None, 2 * NSA_HD, page), lambda bi, c, pt: (l, pt[bi, c * pc + k], 0, 0))

    grid_spec = pltpu.PrefetchScalarGridSpec(
        num_scalar_prefetch=1,
        grid=(db, nch),
        in_specs=[page_spec(k) for k in range(pc)]
        + [pl.BlockSpec((pc * page, pc * per), lambda bi, c, pt: (0, 0))],
        out_specs=pl.BlockSpec((1, 1, 2 * NSA_HD, pc * per), lambda bi, c, pt: (bi, c, 0, 0)),
    )
    out = pl.pallas_call(
        functools.partial(_bmean_sample_body, pc=pc),
        grid_spec=grid_spec,
        out_shape=jax.ShapeDtypeStruct((db, nch, 2 * NSA_HD, pc * per), F32),
        compiler_params=_cparams(("parallel", "arbitrary")),
        name="bmean_sample",
    )(page_table, *([cache_t] * pc), ind)
    return out.transpose(0, 2, 1, 3).reshape(db, 2 * NSA_HD, n_pages * per)


def _cmp_proj_t_body(x_ref, phi_ref, c_ref, s_ref, o_ref):
    half = NSA_HD // 2
    for bi in range(x_ref.shape[0]):
        y = _dot(phi_ref[...], x_ref[bi])
        yk = y[:NSA_HD]
        swapped = jnp.concatenate([yk[half:], yk[:half]], 0)
        o_ref[bi] = jnp.concatenate([yk * c_ref[...] + swapped * s_ref[...], y[NSA_HD:]], 0)


def _cmp_proj_t(means_t, phi_t, cos_t, sin_t, l):
    b, _, nb = means_t.shape
    bt = _pick_tile(b, (8, 4, 2, 1))
    return pl.pallas_call(
        _cmp_proj_t_body,
        grid=(b // bt,),
        in_specs=[
            pl.BlockSpec((bt, LANE, nb), lambda i: (i, 0, 0)),
            pl.BlockSpec((None, LANE, LANE), lambda i: (l, 0, 0)),
            pl.BlockSpec((NSA_HD, nb), lambda i: (0, 0)),
            pl.BlockSpec((NSA_HD, nb), lambda i: (0, 0)),
        ],
        out_specs=pl.BlockSpec((bt, LANE, nb), lambda i: (i, 0, 0)),
        out_shape=jax.ShapeDtypeStruct((b, LANE, nb), F32),
        compiler_params=_cparams(("parallel",)),
        name="cmp_proj_t",
    )(means_t, phi_t, cos_t, sin_t)


def _cmp_proj_body(x_ref, phi_ref, c_ref, s_ref, o_ref):
    bt, nb, _ = x_ref.shape
    y = _dot(x_ref[...].reshape(bt * nb, LANE), phi_ref[...]).reshape(bt, nb, LANE)
    o_ref[...] = _rope_apply(y, c_ref[...][None], s_ref[...][None])


def _cmp_proj(means, phi, cos, sin, l):
    b, nb, _ = means.shape
    bt = _pick_tile(b, (8, 4, 2, 1))
    return pl.pallas_call(
        _cmp_proj_body,
        grid=(b // bt,),
        in_specs=[
            pl.BlockSpec((bt, nb, LANE), lambda i: (i, 0, 0)),
            pl.BlockSpec((None, LANE, LANE), lambda i: (l, 0, 0)),
            pl.BlockSpec((nb, LANE), lambda i: (0, 0)),
            pl.BlockSpec((nb, LANE), lambda i: (0, 0)),
        ],
        out_specs=pl.BlockSpec((bt, nb, LANE), lambda i: (i, 0, 0)),
        out_shape=jax.ShapeDtypeStruct((b, nb, LANE), F32),
        compiler_params=_cparams(("parallel",)),
        name="cmp_proj",
    )(means, phi, cos, sin)


def _topk_select(imp, cand, k, axis):
    nb = imp.shape[axis]
    idx = lax.broadcasted_iota(jnp.int32, imp.shape, axis).astype(F32)
    v = imp if cand is None else jnp.where(cand, imp, -1.0)
    sel = jnp.zeros(imp.shape, F32)
    for _ in range(k):
        m = jnp.max(v, axis, keepdims=True)
        first = jnp.min(jnp.where(v == m, idx, float(nb)), axis, keepdims=True)
        hit = idx == first
        sel = jnp.where(hit & (m >= 0.0), 1.0, sel)
        v = jnp.where(hit, -2.0, v)
    return sel


def _nsa_prompt_body(qt_ref, kbvb_ref, kv_ref, g_ref, o_ref, m_ref, l_ref, acc_ref):
    i = pl.program_id(1)
    heads = NSA_HEADS
    tq = qt_ref.shape[2] // heads
    nb = kbvb_ref.shape[1]
    qt = qt_ref[0]
    kbvb = kbvb_ref[0]

    blk = lax.broadcasted_iota(jnp.int32, (nb, tq), 0)
    qpos_b = i * tq + lax.broadcasted_iota(jnp.int32, (nb, tq), 1)
    vis = jnp.where(blk * NSA_BLOCK + (NSA_BLOCK - 1) <= qpos_b, 1.0, 0.0)
    o_cmp, p = _softmax_once_t(_dot(kbvb[:, :NSA_HD], qt), _lanes(vis, heads), kbvb[:, NSA_HD:].T)
    imp = p[:, :tq]
    for h in range(1, heads):
        imp = imp + p[:, h * tq:(h + 1) * tq]
    cur = qpos_b // NSA_BLOCK
    sel = jnp.where(blk == cur, 1.0, _topk_select(imp, blk < cur, min(NSA_TOPN - 1, nb), 0))

    koff = lax.broadcasted_iota(jnp.int32, (tq, tq), 0)
    qpos = i * tq + lax.broadcasted_iota(jnp.int32, (tq, tq), 1)
    per = tq // NSA_BLOCK
    e_key = lax.broadcasted_iota(jnp.int32, (tq, nb), 0) // NSA_BLOCK
    e_blk = lax.broadcasted_iota(jnp.int32, (tq, nb), 1)

    def sel_step(j, carry):
        kv = kv_ref[0, pl.ds(pl.multiple_of(j * tq, tq), tq), 2 * NSA_HD:4 * NSA_HD]
        chosen = _dot(jnp.where(e_blk == j * per + e_key, 1.0, 0.0), sel)
        bias = jnp.where((chosen > 0.5) & (j * tq + koff <= qpos), 0.0, MASK_BIAS)
        _softmax_step_t(_dot(kv[:, :NSA_HD], qt), _lanes(bias, heads), m_ref, l_ref, acc_ref, kv[:, NSA_HD:].T)
        return carry

    _softmax_init(m_ref, l_ref, acc_ref)
    lax.fori_loop(0, i + 1, sel_step, 0)
    o_sel = acc_ref[...] / l_ref[...]

    def win_step(j, carry):
        kv = kv_ref[0, pl.ds(pl.multiple_of(j * tq, tq), tq), 4 * NSA_HD:6 * NSA_HD]
        rel = qpos - (j * tq + koff)
        bias = jnp.where((rel >= 0) & (rel < NSA_WINDOW), 0.0, MASK_BIAS)
        _softmax_step_t(_dot(kv[:, :NSA_HD], qt), _lanes(bias, heads), m_ref, l_ref, acc_ref, kv[:, NSA_HD:].T)
        return carry

    _softmax_init(m_ref, l_ref, acc_ref)
    lax.fori_loop(jnp.maximum(i - NSA_WINDOW // tq, 0), i + 1, win_step, 0)
    o_win = acc_ref[...] / l_ref[...]

    g_t = g_ref[...].T
    outs = []
    for h in range(heads):
        cols = slice(h * tq, (h + 1) * tq)
        outs.append(g_t[h:h + 1] * o_cmp[:, cols] + g_t[heads + h:heads + h + 1] * o_sel[:, cols]
                    + g_t[2 * heads + h:2 * heads + h + 1] * o_win[:, cols])
    o_ref[...] = jnp.concatenate(outs, 0).T


def _nsa_prompt(q_t, kbvb, kv, gates, b, s):
    tq = Q_TILE
    nq = s // tq
    nb = kbvb.shape[1]
    kvw = kv.shape[-1]
    rows = NSA_HEADS * tq
    return pl.pallas_call(
        _nsa_prompt_body,
        grid=(b, nq),
        in_specs=[
            pl.BlockSpec((1, NSA_HD, rows), lambda bi, i: (bi * nq + i, 0, 0)),
            pl.BlockSpec((1, nb, LANE), lambda bi, i: (bi, 0, 0)),
            pl.BlockSpec((1, s, kvw), lambda bi, i: (bi, 0, 0)),
            pl.BlockSpec((tq, LANE), lambda bi, i: (bi * nq + i, 0)),
        ],
        out_specs=pl.BlockSpec((tq, BRANCH_W), lambda bi, i: (bi * nq + i, 0)),
        out_shape=jax.ShapeDtypeStruct((b * s, BRANCH_W), F32),
        scratch_shapes=[pltpu.VMEM((1, rows), F32), pltpu.VMEM((1, rows), F32), pltpu.VMEM((NSA_HD, rows), F32)],
        compiler_params=_cparams(("parallel", "arbitrary")),
        name="nsa_prompt",
    )(q_t, kbvb, kv.reshape(b, s, kvw), gates)


def _nsa_cmp_sample_body(q_ref, kt_ref, o_ref, sel_ref, *, t_new):
    bt, _, nb = kt_ref.shape
    imps = []
    for bi in range(bt):
        o, p = _softmax_once(_dot(q_ref[bi], kt_ref[bi, :NSA_HD, :]), None, NSA_HEADS, kt_ref[bi, NSA_HD:, :],
                             vt=True)
        o_ref[bi] = o
        imps.append(jnp.sum(p.reshape(NSA_HEADS, t_new, nb), 0))
    sel = _topk_select(jnp.concatenate(imps, 0), None, min(NSA_TOPN - 1, nb), 1)
    sel_ref[...] = sel.reshape(bt, t_new, nb)


def _nsa_cmp_sample(q, kbvb_t, t_new):
    db, rows, _ = q.shape
    nb = kbvb_t.shape[2]
    bt = _pick_tile(db, (8, 4, 2, 1))
    return pl.pallas_call(
        functools.partial(_nsa_cmp_sample_body, t_new=t_new),
        grid=(db // bt,),
        in_specs=[pl.BlockSpec((bt, rows, NSA_HD), lambda bi: (bi, 0, 0)),
                  pl.BlockSpec((bt, LANE, nb), lambda bi: (bi, 0, 0))],
        out_specs=[pl.BlockSpec((bt, rows, NSA_HD), lambda bi: (bi, 0, 0)),
                   pl.BlockSpec((bt, t_new, nb), lambda bi: (bi, 0, 0))],
        out_shape=[jax.ShapeDtypeStruct((db, rows, NSA_HD), F32), jax.ShapeDtypeStruct((db, t_new, nb), F32)],
        compiler_params=_cparams(("parallel",)),
        name="nsa_cmp_sample",
    )(q, kbvb_t)


def _nsa_sel_sample_body(pt_ref, q_ref, sel_ref, exp_ref, *refs, pc, t_new):
    page_refs = refs[:pc]
    new_ref, o_ref, m_ref, l_ref, acc_ref = refs[pc:]
    c = pl.program_id(1)

    @pl.when(c == 0)
    def _():
        _softmax_init(m_ref, l_ref, acc_ref)

    q = q_ref[0]
    page = page_refs[0].shape[1]
    chosen = _dot(sel_ref[0, 0], exp_ref[...])
    state = (m_ref[...], l_ref[...], acc_ref[...])
    for g in range(0, pc, PAGE_GROUP):
        kv_t = jnp.concatenate([r[...].astype(BF16) for r in page_refs[g:g + PAGE_GROUP]], 1)
        state = _softmax_update(state, _dot(q, kv_t[:NSA_HD]), chosen[:, g * page:(g + PAGE_GROUP) * page] > 0.5,
                                NSA_HEADS, kv_t[NSA_HD:], vt=True)
    m_ref[...], l_ref[...], acc_ref[...] = state

    @pl.when(c == pl.num_programs(1) - 1)
    def _():
        kv_new = new_ref[0]
        tq = lax.broadcasted_iota(jnp.int32, (t_new, t_new), 0)
        tk = lax.broadcasted_iota(jnp.int32, (t_new, t_new), 1)
        _softmax_step(_dot_nt(q, kv_new[:, :NSA_HD]), tk <= tq, NSA_HEADS, m_ref, l_ref, acc_ref,
                      kv_new[:, NSA_HD:])
        o_ref[0] = _softmax_out(l_ref, acc_ref)


def _nsa_sel_sample(page_table, q, sel, cache_t, kv_new, l, t_new):
    db, rows, _ = q.shape
    n_pages = page_table.shape[1]
    page = cache_t.shape[3]
    pc = _page_chunk(n_pages)
    per = page // NSA_BLOCK
    nch = n_pages // pc
    sel_c = sel.reshape(db, t_new, nch, pc * per).transpose(0, 2, 1, 3)
    expand = (jnp.arange(pc * per)[:, None] == jnp.arange(pc * page)[None, :] // NSA_BLOCK).astype(BF16)

    def page_spec(k):
        return pl.BlockSpec((None, None, 2 * NSA_HD, page), lambda bi, c, pt: (l, pt[bi, c * pc + k], 1, 0))

    grid_spec = pltpu.PrefetchScalarGridSpec(
        num_scalar_prefetch=1,
        grid=(db, nch),
        in_specs=[pl.BlockSpec((1, rows, NSA_HD), lambda bi, c, pt: (bi, 0, 0)),
                  pl.BlockSpec((1, 1, t_new, pc * per), lambda bi, c, pt: (bi, c, 0, 0)),
                  pl.BlockSpec((pc * per, pc * page), lambda bi, c, pt: (0, 0))]
        + [page_spec(k) for k in range(pc)]
        + [pl.BlockSpec((1, t_new, LANE), lambda bi, c, pt: (bi, 0, 0))],
        out_specs=pl.BlockSpec((1, rows, NSA_HD), lambda bi, c, pt: (bi, 0, 0)),
        scratch_shapes=[pltpu.VMEM((rows, 1), F32), pltpu.VMEM((rows, 1), F32), pltpu.VMEM((rows, NSA_HD), F32)],
    )
    return pl.pallas_call(
        functools.partial(_nsa_sel_sample_body, pc=pc, t_new=t_new),
        grid_spec=grid_spec,
        out_shape=jax.ShapeDtypeStruct((db, rows, NSA_HD), F32),
        compiler_params=_cparams(("parallel", "arbitrary")),
        name="nsa_sel_sample",
    )(page_table, q, sel_c, expand, *([cache_t] * pc), kv_new)


def _nsa_win_sample_body(q_ref, buf_ref, new_ref, ocmp_ref, osel_ref, g_ref, o_ref, *, t_new):
    heads = NSA_HEADS
    lw = buf_ref.shape[2]
    q = q_ref[0]
    buf_t = buf_ref[0]
    kv_new = new_ref[0]
    rel_b = (lax.broadcasted_iota(jnp.int32, (t_new, lw), 0) + lw
             - lax.broadcasted_iota(jnp.int32, (t_new, lw), 1))
    mask_b = (rel_b >= 0) & (rel_b < NSA_WINDOW)
    mask_n = (lax.broadcasted_iota(jnp.int32, (t_new, t_new), 1)
              <= lax.broadcasted_iota(jnp.int32, (t_new, t_new), 0))
    s_b = _masked(_dot(q, buf_t[:NSA_HD]), mask_b, heads, NEG)
    s_n = _masked(_dot_nt(q, kv_new[:, :NSA_HD]), mask_n, heads, NEG)
    m = jnp.maximum(jnp.max(s_b, -1, keepdims=True), jnp.max(s_n, -1, keepdims=True))
    e_b = _masked(jnp.exp(s_b - m), mask_b, heads, 0.0)
    e_n = _masked(jnp.exp(s_n - m), mask_n, heads, 0.0)
    den = jnp.sum(e_b, -1, keepdims=True) + jnp.sum(e_n, -1, keepdims=True)
    o_win = (_dot_nt(e_b, buf_t[NSA_HD:]) + _dot(e_n, kv_new[:, NSA_HD:])) / jnp.where(den > 0, den, 1.0)
    o_cmp = ocmp_ref[0]
    o_sel = osel_ref[0]
    g = g_ref[0]
    for h in range(heads):
        rows = slice(h * t_new, (h + 1) * t_new)
        o_ref[0, :, h * NSA_HD:(h + 1) * NSA_HD] = (
            g[:, h:h + 1] * o_cmp[rows] + g[:, heads + h:heads + h + 1] * o_sel[rows]
            + g[:, 2 * heads + h:2 * heads + h + 1] * o_win[rows])


def _nsa_win_sample(q, buf_t, kv_new, o_cmp, o_sel, gates, t_new):
    db, rows, _ = q.shape
    lw = buf_t.shape[2]
    blk = lambda r, w: pl.BlockSpec((1, r, w), lambda bi: (bi, 0, 0))
    return pl.pallas_call(
        functools.partial(_nsa_win_sample_body, t_new=t_new),
        grid=(db,),
        in_specs=[blk(rows, NSA_HD), blk(LANE, lw), blk(t_new, LANE), blk(rows, NSA_HD), blk(rows, NSA_HD),
                  blk(t_new, LANE)],
        out_specs=blk(t_new, BRANCH_W),
        out_shape=jax.ShapeDtypeStruct((db, t_new, BRANCH_W), F32),
        compiler_params=_cparams(("parallel",)),
        name="nsa_win_sample",
    )(q, buf_t, kv_new, o_cmp, o_sel, gates)


def _rwkv_pre_body(zr_ref, zp_ref, mu_ref, w0_ref, a0_ref, wa_ref, g2_ref, kk_ref, ka_ref, ones_ref,
                   r_ref, k_ref, v_ref, lw_ref, kkn_ref, b_ref, g_ref):
    c = BRANCH_W
    zr = zr_ref[...]
    zs = zr + (zp_ref[...] - zr) * mu_ref[...]
    r, k, v = zs[:, :c], zs[:, c:2 * c], zs[:, 2 * c:3 * c]
    lora = zs[:, 3 * c:3 * c + LANE]
    lane = lax.broadcasted_iota(jnp.int32, lora.shape, 1)
    wa = _dot(jnp.where(lane < RWKV_W_LORA, jnp.tanh(lora), lora), wa_ref[...])
    x = -(w0_ref[...] + wa[:, :c])
    softplus = jnp.maximum(x, 0.0) + jnp.log(1.0 + jnp.exp(-jnp.abs(x)))
    a = jax.nn.sigmoid(a0_ref[...] + wa[:, c:])
    kk = k * kk_ref[...]
    norm = jnp.sqrt(_dot_split(kk * kk, ones_ref[...]))
    kk = kk / jnp.maximum(norm, 1e-12)
    r_ref[...] = r
    k_ref[...] = k * (1.0 + (a - 1.0) * ka_ref[...])
    v_ref[...] = v
    lw_ref[...] = -jnp.exp(-softplus - 0.5)
    kkn_ref[...] = kk
    b_ref[...] = kk * a
    g_ref[...] = _dot(jax.nn.sigmoid(zs[:, 3 * c + LANE:]), g2_ref[...])


def _rwkv_pre(zr, zprev, mu, w0, a0, wa, g2, k_k, k_a, ones_bd, l):
    n = zr.shape[0]
    tm = _pick_tile(n, (256, 128))
    c = BRANCH_W
    row = lambda w: pl.BlockSpec((tm, w), lambda i: (i, 0))
    vec = lambda w: _vec_spec(w, l, 1)
    return pl.pallas_call(
        _rwkv_pre_body,
        grid=(n // tm,),
        in_specs=[row(RWKV_IN), row(RWKV_IN), vec(RWKV_IN), vec(c), vec(c),
                  pl.BlockSpec((None, LANE, 2 * c), lambda i: (l, 0, 0)),
                  pl.BlockSpec((None, RWKV_G_LORA, c), lambda i: (l, 0, 0)),
                  vec(c), vec(c), pl.BlockSpec((c, c), lambda i: (0, 0))],
        out_specs=[row(c)] * 7,
        out_shape=[jax.ShapeDtypeStruct((n, c), F32)] * 7,
        compiler_params=_cparams(("parallel",)),
        name="rwkv_pre",
    )(zr, zprev, _vec3(mu), _vec3(w0), _vec3(a0), wa, g2, _vec3(k_k), _vec3(k_a), ones_bd)


def _rwkv_scan_body(r_ref, k_ref, v_ref, lw_ref, kk_ref, b_ref, s0_ref, y_ref, sf_ref, st_ref, *, c_len, nh):
    ci = pl.program_id(1)

    @pl.when(ci == 0)
    def _():
        st_ref[...] = s0_ref[...]

    for si in range(st_ref.shape[0]):
        _rwkv_scan_seq(si, r_ref, k_ref, v_ref, lw_ref, kk_ref, b_ref, y_ref, st_ref, c_len, nh)

    @pl.when(ci == pl.num_programs(1) - 1)
    def _():
        sf_ref[...] = st_ref[...]


def _rwkv_scan_seq(si, r_ref, k_ref, v_ref, lw_ref, kk_ref, b_ref, y_ref, st_ref, c_len, nh):
    ch = nh * RWKV_HS
    rows = nh * c_len
    lw = lw_ref[si]
    tri = (lax.broadcasted_iota(jnp.int32, (c_len, c_len), 0)
           >= lax.broadcasted_iota(jnp.int32, (c_len, c_len), 1)).astype(BF16)
    lp = None
    rem = lw
    for _ in range(3):
        piece = rem.astype(BF16)
        d = jnp.dot(tri, piece, preferred_element_type=F32)
        lp = d if lp is None else lp + d
        rem = rem - piece.astype(F32)
    p = jnp.exp(lp)
    p_inv = jnp.exp(-lp)
    hm = (lax.broadcasted_iota(jnp.int32, (rows, ch), 0) // c_len
          == lax.broadcasted_iota(jnp.int32, (rows, ch), 1) // RWKV_HS)

    def stack(a):
        return jnp.where(hm, jnp.concatenate([a] * nh, 0), 0.0).astype(BF16)

    x_kk = stack(kk_ref[si] * jnp.exp(lp - lw))
    x_r = stack(r_ref[si] * p)
    y_k = stack(k_ref[si] * p_inv)
    y_b = stack(b_ref[si] * p_inv)
    v = v_ref[si]
    vs = jnp.concatenate([v[:, h * RWKV_HS:(h + 1) * RWKV_HS] for h in range(nh)], 0)

    ti = lax.broadcasted_iota(jnp.int32, (rows, rows), 0) % c_len
    tj = lax.broadcasted_iota(jnp.int32, (rows, rows), 1) % c_len
    strict, incl = ti > tj, ti >= tj
    n_mat = jnp.where(strict, -_dot_nt(x_kk, y_b), 0.0)
    a_kk = jnp.where(strict, _dot_nt(x_kk, y_k), 0.0)
    a_rk = jnp.where(incl, _dot_nt(x_r, y_k), 0.0)
    a_rb = jnp.where(incl, _dot_nt(x_r, y_b), 0.0)
    eye = jnp.where(lax.broadcasted_iota(jnp.int32, (rows, rows), 0)
                    == lax.broadcasted_iota(jnp.int32, (rows, rows), 1), 1.0, 0.0)
    t_inv = eye + n_mat
    pw = n_mat
    span = 2
    while span < c_len:
        pw = _dot(pw, pw)
        t_inv = t_inv + _dot(t_inv, pw)
        span *= 2

    st = st_ref[si]
    u = _dot(t_inv, _dot_nt(x_kk, st) + _dot(a_kk, vs))
    y = _dot_nt(x_r, st) + _dot(a_rk, vs) - _dot(a_rb, u)
    st_ref[si] = p[c_len - 1:c_len, :] * (st + _dot(vs.T, y_k) - _dot(u.T, y_b))
    for h in range(nh):
        y_ref[si, :, h * RWKV_HS:(h + 1) * RWKV_HS] = y[h * c_len:(h + 1) * c_len]


def _rwkv_scan(r, k, v, lw, kk, b, s0, c_len):
    nseq, t, ch = r.shape
    nh = ch // RWKV_HS
    assert nh * c_len == SCAN_ROWS
    ns = _pick_tile(nseq, (2, 1))
    blk = pl.BlockSpec((ns, c_len, ch), lambda si, ci: (si, ci, 0))
    st = pl.BlockSpec((ns, RWKV_HS, ch), lambda si, ci: (si, 0, 0))
    return pl.pallas_call(
        functools.partial(_rwkv_scan_body, c_len=c_len, nh=nh),
        grid=(nseq // ns, t // c_len),
        in_specs=[blk] * 6 + [st],
        out_specs=[blk, st],
        out_shape=[jax.ShapeDtypeStruct((nseq, t, ch), F32), jax.ShapeDtypeStruct((nseq, RWKV_HS, ch), F32)],
        scratch_shapes=[pltpu.VMEM((ns, RWKV_HS, ch), F32)],
        compiler_params=_cparams(("parallel", "arbitrary")),
        name="rwkv_scan",
    )(r, k, v, lw, kk, b, s0)


def _rwkv_post_body(y_ref, r_ref, k_ref, v_ref, g_ref, rk_ref, lnw_ref, lnb_ref, ones_ref, o_ref):
    y = y_ref[...]
    ones = ones_ref[...]
    mu = _dot_split(y, ones) * (1.0 / RWKV_HS)
    dev = y - mu
    var = _dot_split(dev * dev, ones) * (1.0 / RWKV_HS)
    yn = dev * lax.rsqrt(var + RWKV_GN_EPS) * lnw_ref[...] + lnb_ref[...]
    bonus = _dot_split(r_ref[...] * k_ref[...] * rk_ref[...], ones) * v_ref[...]
    o_ref[...] = (yn + bonus) * g_ref[...]


def _rwkv_post(y, r, k, v, g, r_k, ln_w, ln_b, ones_bd, l):
    n, c = y.shape
    tm = _pick_tile(n, (256, 128))
    row = pl.BlockSpec((tm, c), lambda i: (i, 0))
    vec = _vec_spec(c, l, 1)
    return pl.pallas_call(
        _rwkv_post_body,
        grid=(n // tm,),
        in_specs=[row] * 5 + [vec] * 3 + [pl.BlockSpec((c, c), lambda i: (0, 0))],
        out_specs=row,
        out_shape=jax.ShapeDtypeStruct((n, c), F32),
        compiler_params=_cparams(("parallel",)),
        name="rwkv_post",
    )(y, r, k, v, g, _vec3(r_k), _vec3(ln_w), _vec3(ln_b), ones_bd)


def _rope_tables(pos):
    inv = ROPE_THETA ** (-jnp.arange(0, NSA_HD, 2, dtype=F32) / NSA_HD)
    ang = pos.astype(F32)[:, None] * inv[None, :]
    c, s = jnp.cos(ang), jnp.sin(ang)
    return jnp.concatenate([c, c], -1), jnp.concatenate([-s, s], -1)


def _table(cos, sin, groups):
    n = cos.shape[0]
    cs = [cos if rot else jnp.ones((n, w), F32) for w, rot in groups]
    ss = [sin if rot else jnp.zeros((n, w), F32) for w, rot in groups]
    return jnp.concatenate(cs, -1), jnp.concatenate(ss, -1)


def _heads_first(x, tile, group, t_new):
    nt, h, _, d = x.shape
    x = x.reshape(nt, h, tile // t_new, t_new, d).transpose(0, 2, 1, 3, 4)
    return x.reshape(group, h * t_new, d)


def kernel(x_prompt, x_sample, cache_mla, cache_nsa, state_nsa_win, state_pool, state_rwkv, state_rwkv_shift,
           page_table, norm_ffn1, ffn1_w_gate, ffn1_w_up, ffn1_w_down, norm_mix, w_in, w_branch, w_out, pool_w,
           pool_scale, mla_q_norm, mla_w_uq, mla_kv_norm, mla_w_uk, mla_w_uv, nsa_phi_k, nsa_phi_v, rwkv_mu,
           rwkv_w0, rwkv_w2, rwkv_a0, rwkv_a2, rwkv_g2, rwkv_k_k, rwkv_k_a, rwkv_r_k, rwkv_ln_w, rwkv_ln_b,
           norm_ffn2, ffn2_w_gate, ffn2_w_up, ffn2_w_down, norm_final):
    b, s, d = x_prompt.shape
    db, t_new = x_sample.shape[:2]
    depth = norm_ffn1.shape[0]
    n_p, n_s = b * s, db * t_new
    n_pages, page = page_table.shape[1], cache_mla.shape[2]
    past = n_pages * page
    win_len = state_nsa_win.shape[2]
    n_phys = cache_mla.shape[1]
    assert d == D_MODEL and s % Q_TILE == 0 and n_s % Q_TILE == 0 and Q_TILE % t_new == 0
    assert s >= win_len and page % NSA_BLOCK == 0 and t_new < NSA_BLOCK and t_new <= POOL_HALO
    samp_c = t_new
    samp_grp = SCAN_ROWS // (samp_c * RWKV_HEADS)
    assert db % samp_grp == 0 and s % 32 == 0

    bf = lambda w: w.astype(BF16)
    tf = _pick_tile(ffn1_w_gate.shape[2], (512, 256, 128))
    f1 = (bf(_col_tiles(ffn1_w_gate, tf)), bf(_col_tiles(ffn1_w_up, tf)), bf(ffn1_w_down))
    f2 = (bf(_col_tiles(ffn2_w_gate, tf)), bf(_col_tiles(ffn2_w_up, tf)), bf(ffn2_w_down))
    o_mla = BRANCH_W
    o_nsa = o_mla + MLA_Q_LORA + MLA_CACHE_W
    o_rw = o_nsa + BRANCH_W + 6 * NSA_HD + 3 * NSA_HEADS
    o_gate = o_rw + RWKV_IN
    zpad = lambda w: jnp.zeros((depth, d, w), w_in.dtype)
    w_in_p = bf(_col_tiles(jnp.concatenate([
        w_in[:, :, o_gate:], w_in[:, :, :o_mla], w_in[:, :, o_mla:o_mla + MLA_Q_LORA],
        w_in[:, :, o_mla + MLA_Q_LORA:o_nsa], zpad(MLA_ROW_W - MLA_CACHE_W),
        w_in[:, :, o_nsa:o_nsa + BRANCH_W], w_in[:, :, o_nsa + BRANCH_W:o_nsa + BRANCH_W + 6 * NSA_HD],
        w_in[:, :, o_nsa + BRANCH_W + 6 * NSA_HD:o_rw], zpad(LANE - 3 * NSA_HEADS),
        w_in[:, :, o_rw:o_gate], zpad(Z_W - Z_END)], -1), Z_TILE))
    wb = bf(w_branch)
    wo = bf(w_out)
    pw = bf(pool_w)
    uq = mla_w_uq.reshape(depth, MLA_Q_LORA, MLA_HEADS, MLA_NOPE + MLA_ROPE)
    wuq = bf(jnp.concatenate([uq[..., :MLA_NOPE].reshape(depth, MLA_Q_LORA, -1),
                              uq[..., MLA_NOPE:].reshape(depth, MLA_Q_LORA, -1)], -1))
    wuk = bf(mla_w_uk.transpose(0, 2, 3, 1))
    wuv = bf(mla_w_uv.transpose(0, 2, 1, 3))
    zz = jnp.zeros((depth, NSA_HD, NSA_HD), F32)
    phi = bf(jnp.concatenate([jnp.concatenate([nsa_phi_k, zz], -1), jnp.concatenate([zz, nsa_phi_v], -1)], 1))
    phi_t = phi.transpose(0, 2, 1)
    zw =jnp.zeros((depth, RWKV_W_LORA, BRANCH_W), F32)
    wa = bf(jnp.concatenate([jnp.concatenate([rwkv_w2, zw], -1), jnp.concatenate([zw, rwkv_a2], -1)], 1))
    g2 = bf(rwkv_g2)
    r_k = rwkv_r_k.reshape(depth, BRANCH_W)
    ones_bd = (jnp.arange(BRANCH_W)[:, None] // RWKV_HS == jnp.arange(BRANCH_W)[None, :] // RWKV_HS).astype(BF16)

    pos = jnp.concatenate([jnp.tile(jnp.arange(s), b), jnp.tile(past + jnp.arange(t_new), db)])
    cos, sin = _rope_tables(pos)
    tab_mq = _table(cos, sin, [(MLA_HEADS * MLA_NOPE, False)] + [(MLA_ROPE, True)] * MLA_HEADS)
    tab_mkv = _table(cos, sin, [(MLA_KV_LORA, False), (MLA_ROPE, True), (MLA_ROW_W - MLA_CACHE_W, False)])
    tab_nq = _table(cos, sin, [(NSA_HD, True)] * NSA_HEADS)
    tab_nkv = _table(cos, sin, [(2 * NSA_HD, False), (NSA_HD, True), (NSA_HD, False), (NSA_HD, True),
                                (NSA_HD, False)])
    nb_p = s // NSA_BLOCK
    nb_s = past // NSA_BLOCK
    tab_bp = _table(*_rope_tables(jnp.arange(nb_p) * NSA_BLOCK + NSA_BLOCK - 1), [(NSA_HD, True), (NSA_HD, False)])
    tab_bs = [t.T for t in _rope_tables(jnp.arange(nb_s) * NSA_BLOCK + NSA_BLOCK - 1)]

    cache_mla_t = cache_mla.transpose(0, 1, 3, 2)
    cache_nsa_t = cache_nsa.transpose(0, 1, 3, 4, 2).reshape(depth, n_phys, 4 * NSA_HD, page)
    x = jnp.concatenate([x_prompt.reshape(n_p, d), x_sample.reshape(n_s, d)], 0)
    ns_tiles = n_s // Q_TILE
    new_p = [[] for _ in range(6)]
    new_s = [[] for _ in range(6)]
    for l in range(depth):
        x = _ffn(x, norm_ffn1[l], *f1, norm_final, l, False)
        z = _inproj(x, norm_mix[l], w_in_p, l)

        zp = z[:, Z_POOL:Z_POOL + BRANCH_W]
        zp_p = zp[:n_p].reshape(b, s, BRANCH_W)
        zp_s = zp[n_p:].reshape(db, t_new, BRANCH_W)
        pre_p = jnp.zeros((b, POOL_HALO, BRANCH_W), F32)
        pre_s = jnp.concatenate([jnp.zeros((db, POOL_HALO - POOL_BUF, BRANCH_W), F32), state_pool[l]], 1)
        o_pool_p = _pool(jnp.concatenate([pre_p, zp_p], 1), pw, pool_scale, l, 0, 1)
        o_pool_s = _pool(jnp.concatenate([pre_s, zp_s], 1), pw, pool_scale, l, past, _pick_tile(db, (32, 16, 8, 1)))
        o_pool = jnp.concatenate([o_pool_p.reshape(n_p, BRANCH_W), o_pool_s.reshape(n_s, BRANCH_W)], 0)
        new_p[3].append(zp_p[:, -POOL_BUF:])
        new_s[3].append(jnp.concatenate([state_pool[l], zp_s], 1)[:, -POOL_BUF:])

        qcat, qcat_t, rows, rows_b, rows_t = _mla_proj(
            z[:, Z_MQ:Z_MQ + MLA_Q_LORA], z[:, Z_MKV:Z_MKV + MLA_ROW_W], mla_q_norm, mla_kv_norm, wuq, wuk,
            tab_mq + tab_mkv, l)
        o_mla_p = _mla_prompt(qcat_t, rows_b[:n_p], rows_t, wuv, l, b, s)
        q_s = _heads_first(qcat[n_p // Q_TILE:], Q_TILE, db, t_new)
        o_mla_s = _mla_sample(page_table, q_s, cache_mla_t, rows_b[n_p:].reshape(db, t_new, MLA_ROW_W), wuv, l)
        o_mla_all = jnp.concatenate([o_mla_p, o_mla_s.reshape(n_s, BRANCH_W)], 0)
        new_p[0].append(rows[:n_p, :MLA_CACHE_W].reshape(b, s, MLA_CACHE_W))
        new_s[0].append(rows[n_p:, :MLA_CACHE_W].reshape(db, t_new, MLA_CACHE_W))

        nq, nq_t, nkv, ng = _nsa_proj(z[:, Z_NQ:Z_NQ + BRANCH_W], z[:, Z_NKV:Z_NKV + 6 * NSA_HD],
                                      z[:, Z_NG:Z_NG + LANE], tab_nq + tab_nkv)
        kbvb_p = _cmp_proj(_bmean_prompt(nkv[:n_p], b, s), phi, *tab_bp, l)
        o_nsa_p = _nsa_prompt(nq_t, kbvb_p, nkv[:n_p], ng[:n_p], b, s)
        nq_s = _heads_first(nq[n_p // Q_TILE:], Q_TILE, db, t_new)
        nkv_s = nkv[n_p:].reshape(db, t_new, 6 * NSA_HD)
        kbvb_s = _cmp_proj_t(_bmean_sample(page_table, cache_nsa_t, l), phi_t, *tab_bs, l)
        o_cmp_s, sel_s = _nsa_cmp_sample(nq_s, kbvb_s, t_new)
        o_sel_s = _nsa_sel_sample(page_table, nq_s, sel_s, cache_nsa_t, nkv_s[:, :, 2 * NSA_HD:4 * NSA_HD], l, t_new)
        win_buf_t = state_nsa_win[l].transpose(0, 2, 3, 1).reshape(db, 2 * NSA_HD, win_len)
        o_nsa_s = _nsa_win_sample(nq_s, win_buf_t, nkv_s[:, :, 4 * NSA_HD:], o_cmp_s, o_sel_s,
                                  ng[n_p:].reshape(db, t_new, LANE), t_new)
        o_nsa_all = jnp.concatenate([o_nsa_p, o_nsa_s.reshape(n_s, BRANCH_W)], 0)
        nkv_p = nkv[:n_p].reshape(b, s, 6, NSA_HD)
        new_p[1].append(nkv_p[:, :, :4])
        new_s[1].append(nkv_s[:, :, :4 * NSA_HD].reshape(db, t_new, 4, NSA_HD))
        new_p[2].append(nkv_p[:, s - win_len:, 4:])
        new_s[2].append(jnp.concatenate([state_nsa_win[l], nkv_s[:, :, 4 * NSA_HD:].reshape(db, t_new, 2, NSA_HD)],
                                        1)[:, -win_len:])

        zr = z[:, Z_RW:Z_RW + RWKV_IN]
        zr_p = zr[:n_p].reshape(b, s, RWKV_IN)
        zr_s = zr[n_p:].reshape(db, t_new, RWKV_IN)
        zprev = jnp.concatenate([
            jnp.concatenate([jnp.zeros((b, 1, RWKV_IN), F32), zr_p[:, :-1]], 1).reshape(n_p, RWKV_IN),
            jnp.concatenate([state_rwkv_shift[l][:, None], zr_s[:, :-1]], 1).reshape(n_s, RWKV_IN)], 0)
        pre = _rwkv_pre(zr, zprev, rwkv_mu, rwkv_w0, rwkv_a0, wa, g2, rwkv_k_k, rwkv_k_a, ones_bd, l)
        r_, k_, v_, lw_, kk_, b_, g_ = pre
        seq_p = lambda a: a[:n_p].reshape(b, s, BRANCH_W)
        seq_s = lambda a: (a[n_p:].reshape(db // samp_grp, samp_grp, t_new, BRANCH_W).transpose(0, 2, 1, 3)
                           .reshape(db // samp_grp, t_new, samp_grp * BRANCH_W))
        scan_in = (r_, k_, v_, lw_, kk_, b_)
        y_p, sf_p = _rwkv_scan(*[seq_p(a) for a in scan_in],
                               jnp.zeros((b, RWKV_HS, BRANCH_W), F32), SCAN_ROWS // RWKV_HEADS)
        s0_s = (state_rwkv[l].reshape(db // samp_grp, samp_grp, RWKV_HEADS, RWKV_HS, RWKV_HS)
                .transpose(0, 3, 1, 2, 4).reshape(db // samp_grp, RWKV_HS, samp_grp * BRANCH_W))
        y_s, sf_s = _rwkv_scan(*[seq_s(a) for a in scan_in], s0_s, samp_c)
        y_s = (y_s.reshape(db // samp_grp, t_new, samp_grp, BRANCH_W).transpose(0, 2, 1, 3)
               .reshape(n_s, BRANCH_W))
        y_all = jnp.concatenate([y_p.reshape(n_p, BRANCH_W), y_s], 0)
        o_rwkv = _rwkv_post(y_all, r_, k_, v_, g_, r_k, rwkv_ln_w, rwkv_ln_b, ones_bd, l)
        new_p[4].append(sf_p.reshape(b, RWKV_HS, RWKV_HEADS, RWKV_HS).transpose(0, 2, 1, 3))
        new_s[4].append(sf_s.reshape(db // samp_grp, RWKV_HS, samp_grp, RWKV_HEADS, RWKV_HS)
                        .transpose(0, 2, 3, 1, 4).reshape(db, RWKV_HEADS, RWKV_HS, RWKV_HS))
        new_p[5].append(zr_p[:, -1])
        new_s[5].append(zr_s[:, -1])

        x = _merge(x, (o_pool, o_mla_all, o_nsa_all, o_rwkv), z, wb, wo, l)
        x = _ffn(x, norm_ffn2[l], *f2, norm_final, l, l == depth - 1)

    y_prompt = x[:n_p].reshape(b, s, d)
    y_sample = x[n_p:].reshape(db, t_new, d)
    outs_p = [jnp.stack(a, 0) for a in new_p]
    outs_s = [jnp.stack(a, 0) for a in new_s]
    res = [y_prompt, y_sample]
    for a, c in zip(outs_p, outs_s):
        res += [a, c]
    return tuple(res)
```
